```python
import math
import jax
import jax.numpy as jnp
from jax import lax
import numpy as np

D_MODEL = 2048
BATCH = 4
SEQ = 4096
DEPTH = 4

GLA_H = 4
GLA_DK = 64
GLA_DV = 128
GLA_GATE_RANK = 16
GLA_GATE_TAU = 16.0
GLA_CHUNK = 16
SSD_DINNER = 512
SSD_HD = 64
SSD_H = SSD_DINNER // SSD_HD
SSD_G = 2
SSD_N = 128
SSD_CONV = 4
SSD_CONV_DIM = SSD_DINNER + 2 * SSD_G * SSD_N
SSD_CHUNK = 64
MLA_H = 4
MLA_NOPE = 128
MLA_ROPE = 64
MLA_QK = MLA_NOPE + MLA_ROPE
MLA_V = 128
MLA_Q_LORA = 384
MLA_KV_LORA = 128
DIL_H = 4
DIL_HD = 128
DIL_CONFIGS = ((128, 1), (512, 4), (2048, 16))
DIL_MAX_W = 2048

Q_BLOCK = 128
ROPE_THETA = 10000.0
EPS = 1e-6
D_MIX = GLA_H * GLA_DV + SSD_DINNER + MLA_H * MLA_V + DIL_H * DIL_HD
IN_WIDTHS = (GLA_H * GLA_DV, SSD_DINNER, MLA_H * MLA_V, DIL_H * DIL_HD,
             GLA_H * GLA_DK, GLA_H * GLA_DK, GLA_H * GLA_DV, GLA_GATE_RANK,
             SSD_CONV_DIM, SSD_H,
             MLA_Q_LORA, MLA_KV_LORA, MLA_ROPE,
             DIL_H * DIL_HD, DIL_H * DIL_HD, DIL_H * DIL_HD)
N_IN = sum(IN_WIDTHS)

kernel_name = 'hybrid_gla_ssd_mla_dilated_parallel_heads'


def _split(u, widths):
    out, start = [], 0
    for w in widths:
        out.append(u[..., start:start + w])
        start += w
    return out


def _rmsnorm(x, g):
    xf = x.astype(jnp.float32)
    y = xf * lax.rsqrt(jnp.mean(xf * xf, axis=-1, keepdims=True) + EPS)
    return (y * g.astype(jnp.float32)).astype(x.dtype)


def _rope(x, pos):
    d = x.shape[-1]
    half = d // 2
    inv_freq = jnp.exp(-math.log(ROPE_THETA) * jnp.arange(half, dtype=jnp.float32) * (2.0 / d))
    ang = pos.astype(jnp.float32)[:, :, None] * inv_freq
    cos = jnp.cos(ang)[:, :, None, :]
    sin = jnp.sin(ang)[:, :, None, :]
    xf = x.astype(jnp.float32)
    x1, x2 = xf[..., :half], xf[..., half:]
    return jnp.concatenate([x1 * cos - x2 * sin, x2 * cos + x1 * sin], axis=-1).astype(x.dtype)


def _gla(q, k, v, log_a, norm_g):
    Bsz, S, H, DK = q.shape
    DV = v.shape[-1]
    C = GLA_CHUNK
    N = S // C

    def chunks(t):
        return t.astype(jnp.float32).reshape(Bsz, N, C, H, t.shape[-1]).transpose(0, 3, 1, 2, 4)

    qc = chunks(q) * (DK ** -0.5)
    kc = chunks(k)
    vc = chunks(v)
    bc = jnp.cumsum(chunks(log_a), axis=3)
    causal = jnp.tril(jnp.ones((C, C), bool))[:, :, None]
    diff = bc[:, :, :, :, None, :] - bc[:, :, :, None, :, :]
    decay = jnp.exp(jnp.where(causal, diff, -jnp.inf))
    attn = jnp.einsum('bhnid,bhnjd,bhnijd->bhnij', qc, kc, decay)
    o_intra = jnp.einsum('bhnij,bhnjv->bhniv', attn, vc)
    b_last = bc[:, :, :, -1, :]
    upd = jnp.einsum('bhncd,bhncv->bhndv', kc * jnp.exp(b_last[:, :, :, None, :] - bc), vc)

    def step(state, inp):
        dec, u = inp
        return state * dec[..., None] + u, state

    init = jnp.zeros((Bsz, H, DK, DV), jnp.float32)
    _, s_before = lax.scan(step, init, (jnp.moveaxis(jnp.exp(b_last), 2, 0), jnp.moveaxis(upd, 2, 0)))
    s_before = jnp.moveaxis(s_before, 0, 2)
    o_inter = jnp.einsum('bhncd,bhndv->bhncv', qc * jnp.exp(bc), s_before)
    o = (o_intra + o_inter).transpose(0, 2, 3, 1, 4).reshape(Bsz, S, H, DV)
    return _rmsnorm(o, norm_g).reshape(Bsz, S, H * DV)


def _segsum(x):
    T = x.shape[-1]
    xe = jnp.broadcast_to(x[..., :, None], x.shape + (T,))
    xe = jnp.where(jnp.tril(jnp.ones((T, T), bool), -1), xe, 0.0)
    ss = jnp.cumsum(xe, axis=-2)
    return jnp.where(jnp.tril(jnp.ones((T, T), bool)), ss, -jnp.inf)


def _ssd_scan(X, A, Bm, Cm):
    b, s, h, p = X.shape
    n = Bm.shape[-1]
    L = SSD_CHUNK
    c = s // L
    Xc = X.reshape(b, c, L, h, p)
    Bc = Bm.reshape(b, c, L, h, n)
    Cc = Cm.reshape(b, c, L, h, n)
    Ac = A.reshape(b, c, L, h).transpose(0, 3, 1, 2)
    A_cs = jnp.cumsum(Ac, axis=-1)
    Lm = jnp.exp(_segsum(Ac))
    y_diag = jnp.einsum('bclhn,bcshn,bhcls,bcshp->bclhp', Cc, Bc, Lm, Xc)
    decay_states = jnp.exp(A_cs[..., -1:] - A_cs)
    states = jnp.einsum('bclhn,bhcl,bclhp->bchpn', Bc, decay_states, Xc)

    def step(st, inp):
        dec, s_new = inp
        return st * dec[:, :, None, None] + s_new, st

    init = jnp.zeros((b, h, p, n), jnp.float32)
    _, prev = lax.scan(step, init, (jnp.moveaxis(jnp.exp(A_cs[..., -1]), 2, 0), jnp.moveaxis(states, 1, 0)))
    prev = jnp.moveaxis(prev, 0, 1)
    y_off = jnp.einsum('bclhn,bchpn,bhcl->bclhp', Cc, prev, jnp.exp(A_cs))
    return (y_diag + y_off).reshape(b, s, h, p)


def _ssd_mixer(xbc, dt_raw, z, conv_w, conv_b, dt_bias, A_log, D, norm_g):
    Bsz, S, _ = xbc.shape
    xpad = jnp.pad(xbc.astype(jnp.float32), ((0, 0), (SSD_CONV - 1, 0), (0, 0)))
    conv = jnp.broadcast_to(conv_b.astype(jnp.float32), (Bsz, S, SSD_CONV_DIM))
    for j in range(SSD_CONV):
        conv = conv + xpad[:, j:j + S, :] * conv_w[j].astype(jnp.float32)
    xbc = jax.nn.silu(conv)
    xs, Bm, Cm = _split(xbc, (SSD_DINNER, SSD_G * SSD_N, SSD_G * SSD_N))
    xs = xs.reshape(Bsz, S, SSD_H, SSD_HD)
    Bm = jnp.repeat(Bm.reshape(Bsz, S, SSD_G, SSD_N), SSD_H // SSD_G, axis=2)
    Cm = jnp.repeat(Cm.reshape(Bsz, S, SSD_G, SSD_N), SSD_H // SSD_G, axis=2)
    dt = jax.nn.softplus(dt_raw.astype(jnp.float32) + dt_bias.astype(jnp.float32))
    A = -jnp.exp(A_log.astype(jnp.float32))
    y = _ssd_scan(xs * dt[..., None], dt * A, Bm, Cm) + xs * D.astype(jnp.float32)[:, None]
    y = y.reshape(Bsz, S, SSD_DINNER) * jax.nn.silu(z.astype(jnp.float32))
    y = _rmsnorm(y.reshape(Bsz, S, SSD_G, SSD_DINNER // SSD_G), norm_g.reshape(SSD_G, SSD_DINNER // SSD_G))
    return y.reshape(Bsz, S, SSD_DINNER)


def _causal_attention(q, k, v):
    Bsz, S, H, dq = q.shape
    nb = S // Q_BLOCK
    scale = dq ** -0.5
    qb = q.reshape(Bsz, nb, Q_BLOCK, H, dq).transpose(1, 0, 2, 3, 4)
    kpos = jnp.arange(S)

    def one(args):
        qi, i = args
        s = jnp.einsum('bqhd,bkhd->bhqk', qi, k, preferred_element_type=jnp.float32) * scale
        qpos = i * Q_BLOCK + jnp.arange(Q_BLOCK)
        s = jnp.where(kpos[None, :] <= qpos[:, None], s, -jnp.inf)
        p = jax.nn.softmax(s, axis=-1)
        return jnp.einsum('bhqk,bkhd->bqhd', p.astype(v.dtype), v)

    o = lax.map(one, (qb, jnp.arange(nb)))
    return o.transpose(1, 0, 2, 3, 4).reshape(Bsz, S, H, v.shape[-1])


def _mla(c_q, c_kv, k_rope, positions, q_norm_g, kv_norm_g, w_uq, w_ukv, q_head_g, k_head_g):
    Bsz, S, _ = c_q.shape
    q = (_rmsnorm(c_q, q_norm_g) @ w_uq).reshape(Bsz, S, MLA_H, MLA_QK)
    kv = (_rmsnorm(c_kv, kv_norm_g) @ w_ukv).reshape(Bsz, S, MLA_H, MLA_NOPE + MLA_V)
    k_nope, v = kv[..., :MLA_NOPE], kv[..., MLA_NOPE:]
    k_r = jnp.broadcast_to(k_rope[:, :, None, :], (Bsz, S, MLA_H, MLA_ROPE))
    k = _rmsnorm(jnp.concatenate([k_nope, k_r], axis=-1), k_head_g)
    q = _rmsnorm(q, q_head_g)
    q = jnp.concatenate([q[..., :MLA_NOPE], _rope(q[..., MLA_NOPE:], positions)], axis=-1)
    k = jnp.concatenate([k[..., :MLA_NOPE], _rope(k[..., MLA_NOPE:], positions)], axis=-1)
    return _causal_attention(q, k, v).reshape(Bsz, S, MLA_H * MLA_V)


def _dilated_attention(q, k, v):
    Bsz, S, H, Dh = q.shape
    nb = S // Q_BLOCK
    scale = Dh ** -0.5
    pad = ((0, 0), (DIL_MAX_W, 0), (0, 0), (0, 0))
    kp = jnp.pad(k, pad)
    vp = jnp.pad(v, pad)
    qb = q.reshape(Bsz, nb, Q_BLOCK, H, Dh).transpose(1, 0, 2, 3, 4)
    rows = np.arange(Q_BLOCK)

    def one(args):
        qi, i = args
        t0 = i * Q_BLOCK
        scores, vals, sizes = [], [], []
        for (w, d) in DIL_CONFIGS:
            taps = np.arange(w // d + 1)
            idx = rows[:, None] + w - taps[None, :] * d
            start = t0 + DIL_MAX_W - w
            ks = lax.dynamic_slice_in_dim(kp, start, Q_BLOCK + w, axis=1)
            vs = lax.dynamic_slice_in_dim(vp, start, Q_BLOCK + w, axis=1)
            kg = jnp.take(ks, idx, axis=1)
            s = jnp.einsum('bqhd,bqjhd->bhqj', qi, kg, preferred_element_type=jnp.float32) * scale
            valid = (t0 + rows[:, None] - taps[None, :] * d) >= 0
            scores.append(jnp.where(valid, s, -jnp.inf))
            vals.append(jnp.take(vs, idx, axis=1))
            sizes.append(taps.shape[0])
        p = jax.nn.softmax(jnp.concatenate(scores, axis=-1), axis=-1)
        out = None
        for p_c, v_c in zip(_split(p, sizes), vals):
            o_c = jnp.einsum('bhqj,bqjhd->bqhd', p_c.astype(v_c.dtype), v_c)
            out = o_c if out is None else out + o_c
        return out

    o = lax.map(one, (qb, jnp.arange(nb)))
    return o.transpose(1, 0, 2, 3, 4).reshape(Bsz, S, H * Dh)


def setup_inputs(seed: int = 0) -> dict:
    key = jax.random.key(seed)
    ks = jax.random.split(key, 24)

    def nrm(k, shape, scale):
        return jax.random.normal(k, shape, jnp.float32) * scale

    def gain(k, n):
        return 1.0 + nrm(k, (DEPTH, n), 0.02)

    x = nrm(ks[0], (BATCH, SEQ, D_MODEL), 1.0)
    offs = jax.random.randint(ks[1], (BATCH, 1), 0, SEQ, dtype=jnp.int32)
    positions = offs + jnp.arange(SEQ, dtype=jnp.int32)[None, :]
    dt0 = jnp.exp(jax.random.uniform(ks[2], (DEPTH, SSD_H), jnp.float32, math.log(1e-3), math.log(1e-1)))
    return {
        'x': x,
        'positions': positions,
        'ln_g': gain(ks[3], D_MODEL),
        'w_in': nrm(ks[4], (DEPTH, D_MODEL, N_IN), D_MODEL ** -0.5),
        'w_out': nrm(ks[5], (DEPTH, D_MIX, D_MODEL), D_MIX ** -0.5),
        'gla_gate_w2': nrm(ks[6], (DEPTH, GLA_GATE_RANK, GLA_H * GLA_DK), GLA_GATE_RANK ** -0.5),
        'gla_gate_b': nrm(ks[7], (DEPTH, GLA_H * GLA_DK), 0.1),
        'gla_norm_g': gain(ks[8], GLA_DV),
        'ssd_conv_w': nrm(ks[9], (DEPTH, SSD_CONV, SSD_CONV_DIM), SSD_CONV ** -0.5),
        'ssd_conv_b': nrm(ks[10], (DEPTH, SSD_CONV_DIM), 0.02),
        'ssd_dt_bias': dt0 + jnp.log(-jnp.expm1(-dt0)),
        'ssd_A_log': jnp.log(jax.random.uniform(ks[11], (DEPTH, SSD_H), jnp.float32, 1.0, 16.0)),
        'ssd_D': 1.0 + nrm(ks[12], (DEPTH, SSD_H), 0.1),
        'ssd_norm_g': gain(ks[13], SSD_DINNER),
        'mla_q_norm_g': gain(ks[14], MLA_Q_LORA),
        'mla_kv_norm_g': gain(ks[15], MLA_KV_LORA),
        'mla_w_uq': nrm(ks[16], (DEPTH, MLA_Q_LORA, MLA_H * MLA_QK), MLA_Q_LORA ** -0.5),
        'mla_w_ukv': nrm(ks[17], (DEPTH, MLA_KV_LORA, MLA_H * (MLA_NOPE + MLA_V)), MLA_KV_LORA ** -0.5),
        'mla_q_head_g': gain(ks[18], MLA_QK),
        'mla_k_head_g': gain(ks[19], MLA_QK),
        'dil_q_g': gain(ks[20], DIL_HD),
        'dil_k_g': gain(ks[21], DIL_HD),
    }


def reference(x, positions, ln_g, w_in, w_out, gla_gate_w2, gla_gate_b, gla_norm_g,
              ssd_conv_w, ssd_conv_b, ssd_dt_bias, ssd_A_log, ssd_D, ssd_norm_g,
              mla_q_norm_g, mla_kv_norm_g, mla_w_uq, mla_w_ukv, mla_q_head_g, mla_k_head_g,
              dil_q_g, dil_k_g):
    Bsz, S, _ = x.shape
    for l in range(DEPTH):
        h = _rmsnorm(x, ln_g[l])
        u = h @ w_in[l]
        (g_a, g_b, g_c, g_d, a_q, a_k, a_v, a_lr, b_xbc, b_dt,
         c_q, c_kv, c_kr, d_q, d_k, d_v) = _split(u, IN_WIDTHS)
        log_a = jax.nn.log_sigmoid((a_lr @ gla_gate_w2[l] + gla_gate_b[l]).astype(jnp.float32)) / GLA_GATE_TAU
        y_a = _gla(a_q.reshape(Bsz, S, GLA_H, GLA_DK), a_k.reshape(Bsz, S, GLA_H, GLA_DK),
                   a_v.reshape(Bsz, S, GLA_H, GLA_DV), log_a.reshape(Bsz, S, GLA_H, GLA_DK),
                   gla_norm_g[l]) * jax.nn.silu(g_a)
        y_b = _ssd_mixer(b_xbc, b_dt, g_b, ssd_conv_w[l], ssd_conv_b[l], ssd_dt_bias[l],
                         ssd_A_log[l], ssd_D[l], ssd_norm_g[l])
        y_c = _mla(c_q, c_kv, c_kr, positions, mla_q_norm_g[l], mla_kv_norm_g[l], mla_w_uq[l],
                   mla_w_ukv[l], mla_q_head_g[l], mla_k_head_g[l]) * jax.nn.silu(g_c)
        dq = _rope(_rmsnorm(d_q.reshape(Bsz, S, DIL_H, DIL_HD), dil_q_g[l]), positions)
        dk = _rope(_rmsnorm(d_k.reshape(Bsz, S, DIL_H, DIL_HD), dil_k_g[l]), positions)
        y_d = _dilated_attention(dq, dk, d_v.reshape(Bsz, S, DIL_H, DIL_HD)) * jax.nn.silu(g_d)
        y = jnp.concatenate([y_a, y_b, y_c, y_d], axis=-1).astype(x.dtype)
        x = x + y @ w_out[l]
    return x
```

```python
import functools
import math

import jax
import jax.numpy as jnp
import numpy as np
from jax import lax
from jax.experimental import pallas as pl
from jax.experimental.pallas import tpu as pltpu

F32 = jnp.float32
BF16 = jnp.bfloat16

D_MODEL = 2048
GLA_H, GLA_DK, GLA_DV = 4, 64, 128
GLA_GATE_RANK = 16
GLA_GATE_TAU = 16.0
SSD_DINNER, SSD_HD, SSD_G, SSD_N, SSD_CONV = 512, 64, 2, 128, 4
SSD_H = SSD_DINNER // SSD_HD
SSD_CONV_DIM = SSD_DINNER + 2 * SSD_G * SSD_N
MLA_H, MLA_NOPE, MLA_ROPE, MLA_V = 4, 128, 64, 128
MLA_QK = MLA_NOPE + MLA_ROPE
MLA_Q_LORA, MLA_KV_LORA = 384, 128
DIL_H, DIL_HD = 4, 128
DIL_CONFIGS = ((128, 1), (512, 4), (2048, 16))
ROPE_THETA = 10000.0
EPS = 1e-6
D_MIX = GLA_H * GLA_DV + SSD_DINNER + MLA_H * MLA_V + DIL_H * DIL_HD

LANE = 128
VMEM_LIMIT = 52 * 1024 * 1024

COL_GA, COL_GB, COL_GC, COL_GD = 0, 512, 1024, 1536
COL_AQK = 2048
COL_AV = 2560
COL_XBC = 3072
COL_CQKV = 4096
COL_DQ, COL_DK, COL_DV = 4608, 5120, 5632
COL_SMALL = 6144
N_PACK = 6272
SM_KR, SM_LR, SM_DT = 0, 64, 80

GLA_CHUNK = 32
GLA_ROWS = 256
SSD_CHUNK = 256
DIL_BLK = 128


def _cparams(sem):
    return pltpu.CompilerParams(dimension_semantics=sem, vmem_limit_bytes=VMEM_LIMIT)


def _silu(x):
    return x * (1.0 / (1.0 + jnp.exp(-x)))


def _dot(a, b):
    return jnp.dot(a, b, preferred_element_type=F32)


def _dot_nt(a, b):
    return lax.dot_general(a, b, (((1,), (1,)), ((), ())), preferred_element_type=F32)


def _dot_tn(a, b):
    return lax.dot_general(a, b, (((0,), (0,)), ((), ())), preferred_element_type=F32)


def _split_dot(tri, x):
    hi = x.astype(BF16)
    lo = (x - hi.astype(F32)).astype(BF16)
    return _dot(tri, hi) + _dot(tri, lo)


def _rope_tables_kernel(pos_ref, fm_ref, fd_ref, cm_ref, s1_ref, s2_ref, cd_ref, sd_ref):
    pos = pos_ref[...]
    lane = lax.broadcasted_iota(jnp.int32, (1, LANE), 1)
    ang_m = pos * fm_ref[...]
    cos_m, sin_m = jnp.cos(ang_m), jnp.sin(ang_m)
    cm_ref[...] = jnp.where(lane < MLA_ROPE, cos_m, 0.0)
    s1_ref[...] = jnp.where(lane < MLA_ROPE // 2, -sin_m, 0.0)
    s2_ref[...] = jnp.where((lane >= MLA_ROPE // 2) & (lane < MLA_ROPE), sin_m, 0.0)
    ang_d = pos * fd_ref[...]
    cd_ref[...] = jnp.cos(ang_d)
    sd_ref[...] = jnp.where(lane < DIL_HD // 2, -jnp.sin(ang_d), jnp.sin(ang_d))


def _rope_tables(positions):
    T = positions.size
    ts = min(T, 2048)
    pos = positions.reshape(T, 1).astype(F32)
    lane = np.arange(LANE)
    fm = np.exp(-math.log(ROPE_THETA) * (lane % (MLA_ROPE // 2)) * (2.0 / MLA_ROPE))
    fd = np.exp(-math.log(ROPE_THETA) * (lane % (DIL_HD // 2)) * (2.0 / DIL_HD))
    fm = jnp.asarray(fm, F32).reshape(1, LANE)
    fd = jnp.asarray(fd, F32).reshape(1, LANE)
    row = pl.BlockSpec((ts, LANE), lambda i: (i, 0))
    const = pl.BlockSpec((1, LANE), lambda i: (0, 0))
    return pl.pallas_call(
        _rope_tables_kernel,
        grid=(T // ts,),
        in_specs=[pl.BlockSpec((ts, 1), lambda i: (i, 0)), const, const],
        out_specs=[row] * 5,
        out_shape=[jax.ShapeDtypeStruct((T, LANE), F32)] * 5,
        compiler_params=_cparams(("parallel",)),
        name="rope_tables",
    )(pos, fm, fd)


def _inproj_kernel(x_ref, g_ref, w_ref, o_ref, h_ref):
    @pl.when(pl.program_id(1) == 0)
    def _():
        x = x_ref[...]
        ms = jnp.mean(x * x, axis=-1, keepdims=True)
        h_ref[...] = (x * lax.rsqrt(ms + EPS) * g_ref[...]).astype(BF16)

    o_ref[...] = _dot(h_ref[...], w_ref[...])


def _inproj(x, g, w, tm=512, tn=896):
    T = x.shape[0]
    tm = min(tm, T)
    return pl.pallas_call(
        _inproj_kernel,
        grid=(T // tm, N_PACK // tn),
        in_specs=[
            pl.BlockSpec((tm, D_MODEL), lambda i, j: (i, 0)),
            pl.BlockSpec((1, D_MODEL), lambda i, j: (0, 0)),
            pl.BlockSpec((D_MODEL, tn), lambda i, j: (0, j)),
        ],
        out_specs=pl.BlockSpec((tm, tn), lambda i, j: (i, j)),
        out_shape=jax.ShapeDtypeStruct((T, N_PACK), F32),
        scratch_shapes=[pltpu.VMEM((tm, D_MODEL), BF16)],
        compiler_params=_cparams(("parallel", "arbitrary")),
        name="inproj",
    )(x, g, w)


def _outproj_kernel(x_ref, ya_ref, yb_ref, yc_ref, yd_ref, w_ref, o_ref):
    acc = x_ref[...]
    for n, y_ref in enumerate((ya_ref, yb_ref, yc_ref, yd_ref)):
        acc = acc + _dot(y_ref[...], w_ref[n * 512:(n + 1) * 512, :])
    o_ref[...] = acc


def _outproj(x, ya, yb, yc, yd, w, tm=256):
    T = x.shape[0]
    tm = min(tm, T)
    yspec = pl.BlockSpec((tm, 512), lambda i: (i, 0))
    return pl.pallas_call(
        _outproj_kernel,
        grid=(T // tm,),
        in_specs=[pl.BlockSpec((tm, D_MODEL), lambda i: (i, 0)), yspec, yspec, yspec, yspec,
                  pl.BlockSpec((D_MIX, D_MODEL), lambda i: (0, 0))],
        out_specs=pl.BlockSpec((tm, D_MODEL), lambda i: (i, 0)),
        out_shape=jax.ShapeDtypeStruct((T, D_MODEL), F32),
        compiler_params=_cparams(("parallel",)),
        name="outproj",
    )(x, ya, yb, yc, yd, w)


def _gla_kernel(qk_ref, v_ref, gate_ref, sm_ref, w2_ref, b2_ref, ng_ref, o_ref, st_ref):
    @pl.when(pl.program_id(1) == 0)
    def _():
        st_ref[...] = jnp.zeros_like(st_ref)

    R, C = GLA_ROWS, GLA_CHUNK
    xg = _dot(sm_ref[...].astype(BF16), w2_ref[...]) + b2_ref[...]
    logd = (jnp.minimum(xg, 0.0) - jnp.log(1.0 + jnp.exp(-jnp.abs(xg)))) * (1.0 / GLA_GATE_TAU)
    ri = lax.broadcasted_iota(jnp.int32, (R, R), 0)
    ci = lax.broadcasted_iota(jnp.int32, (R, R), 1)
    back = ri - ci
    tri = jnp.where((back >= 0) & (back <= (ri & (C - 1))), 1.0, 0.0).astype(BF16)
    bc_all = _split_dot(tri, logd)
    causal = lax.broadcasted_iota(jnp.int32, (C, C), 1) <= lax.broadcasted_iota(jnp.int32, (C, C), 0)
    scale = GLA_DK ** -0.5
    ng = ng_ref[...]
    for c in range(R // C):
        rows = slice(c * C, (c + 1) * C)
        for h in range(GLA_H):
            kcols = slice(h * GLA_DK, (h + 1) * GLA_DK)
            bc = bc_all[rows, kcols]
            q = qk_ref[rows, kcols] * scale
            k = qk_ref[rows, GLA_H * GLA_DK + h * GLA_DK:GLA_H * GLA_DK + (h + 1) * GLA_DK]
            v = v_ref[rows, h * GLA_DV:(h + 1) * GLA_DV].astype(BF16)
            b_mid = bc[C // 2 - 1:C // 2, :]
            b_last = bc[C - 1:C, :]
            qd = (q * jnp.exp(bc - b_mid)).astype(BF16)
            kd = (k * jnp.exp(b_mid - bc)).astype(BF16)
            attn = jnp.where(causal, _dot_nt(qd, kd), 0.0)
            st = st_ref[h]
            o = _dot(attn.astype(BF16), v) + _dot_nt((q * jnp.exp(bc)).astype(BF16), st.astype(BF16))
            kl = (k * jnp.exp(b_last - bc)).astype(BF16)
            st_ref[h] = st * jnp.exp(b_last) + _dot_tn(v, kl)
            ms = jnp.mean(o * o, axis=-1, keepdims=True)
            vcols = slice(h * GLA_DV, (h + 1) * GLA_DV)
            y = o * lax.rsqrt(ms + EPS) * ng * _silu(gate_ref[rows, vcols])
            o_ref[rows, vcols] = y.astype(BF16)


def _gla(u, w2p, b2, ng, B, S):
    R = GLA_ROWS
    nb = S // R
    row = lambda c: (lambda b, i: (b * nb + i, c))
    const = lambda b, i: (0, 0)
    return pl.pallas_call(
        _gla_kernel,
        grid=(B, nb),
        in_specs=[
            pl.BlockSpec((R, 512), row(COL_AQK // 512)),
            pl.BlockSpec((R, 512), row(COL_AV // 512)),
            pl.BlockSpec((R, 512), row(COL_GA // 512)),
            pl.BlockSpec((R, LANE), row(COL_SMALL // LANE)),
            pl.BlockSpec((LANE, GLA_H * GLA_DK), const),
            pl.BlockSpec((1, GLA_H * GLA_DK), const),
            pl.BlockSpec((1, GLA_DV), const),
        ],
        out_specs=pl.BlockSpec((R, 512), row(0)),
        out_shape=jax.ShapeDtypeStruct((B * S, GLA_H * GLA_DV), BF16),
        scratch_shapes=[pltpu.VMEM((GLA_H, GLA_DV, GLA_DK), F32)],
        compiler_params=_cparams(("parallel", "arbitrary")),
        name="gla",
    )(u, u, u, u, w2p, b2, ng)


def _ssd_kernel(xbc_ref, z_ref, sm_ref, cw_ref, cb_ref, dtb_ref, a_ref, d_ref, ng_ref, o_ref,
                xpad_ref, st_ref, y_ref):
    L = SSD_CHUNK
    HALO = 8

    @pl.when(pl.program_id(1) == 0)
    def _():
        st_ref[...] = jnp.zeros_like(st_ref)
        xpad_ref[0:HALO, :] = jnp.zeros((HALO, SSD_CONV_DIM), F32)

    xpad_ref[HALO:HALO + L, :] = xbc_ref[...]
    conv = cb_ref[...] + jnp.zeros((L, SSD_CONV_DIM), F32)
    for j in range(SSD_CONV):
        conv = conv + xpad_ref[pl.ds(HALO - (SSD_CONV - 1) + j, L), :] * cw_ref[j:j + 1, :]
    xpad_ref[0:HALO, :] = xpad_ref[L:L + HALO, :]
    xact = _silu(conv)

    lane = lax.broadcasted_iota(jnp.int32, (1, LANE), 1)
    is_dt = (lane >= SM_DT) & (lane < SM_DT + SSD_H)
    pre = sm_ref[...] + dtb_ref[...]
    dt_all = jnp.maximum(pre, 0.0) + jnp.log(1.0 + jnp.exp(-jnp.abs(pre)))
    a_all = jnp.where(is_dt, dt_all * -jnp.exp(a_ref[...]), 0.0)
    ri = lax.broadcasted_iota(jnp.int32, (L, L), 0)
    ci = lax.broadcasted_iota(jnp.int32, (L, L), 1)
    lower = ci <= ri
    cs_all = _split_dot(jnp.where(lower, 1.0, 0.0).astype(BF16), a_all)
    cs_t = cs_all.T

    heads_per_group = SSD_H // SSD_G
    for g in range(SSD_G):
        bm = xact[:, SSD_DINNER + g * SSD_N:SSD_DINNER + (g + 1) * SSD_N].astype(BF16)
        cm = xact[:, SSD_DINNER + SSD_G * SSD_N + g * SSD_N:
                  SSD_DINNER + SSD_G * SSD_N + (g + 1) * SSD_N].astype(BF16)
        scores = _dot_nt(cm, bm)
        for hh in range(heads_per_group):
            h = g * heads_per_group + hh
            xs = xact[:, h * SSD_HD:(h + 1) * SSD_HD]
            dt = dt_all[:, SM_DT + h:SM_DT + h + 1]
            cs_col = cs_all[:, SM_DT + h:SM_DT + h + 1]
            cs_row = cs_t[SM_DT + h:SM_DT + h + 1, :]
            cs_last = cs_col[L - 1:L, :]
            decay = jnp.where(lower, jnp.exp(jnp.minimum(cs_col - cs_row, 0.0)), 0.0)
            xdt = xs * dt
            y = _dot((scores * decay).astype(BF16), xdt.astype(BF16))
            prev = st_ref[h]
            y = y + _dot_nt(cm, prev.astype(BF16)) * jnp.exp(cs_col)
            st_ref[h] = prev * jnp.exp(cs_last) + _dot_tn(
                (xdt * jnp.exp(cs_last - cs_col)).astype(BF16), bm)
            y_ref[:, h * SSD_HD:(h + 1) * SSD_HD] = y + xs * d_ref[:, h * SSD_HD:(h + 1) * SSD_HD]

    yz = y_ref[...] * _silu(z_ref[...])
    gw = SSD_DINNER // SSD_G
    for g in range(SSD_G):
        yg = yz[:, g * gw:(g + 1) * gw]
        ms = jnp.mean(yg * yg, axis=-1, keepdims=True)
        o_ref[:, g * gw:(g + 1) * gw] = (yg * lax.rsqrt(ms + EPS) * ng_ref[:, g * gw:(g + 1) * gw]).astype(BF16)


def _ssd(u, cw, cb, dtb, a_pad, d_exp, ng, B, S):
    L = SSD_CHUNK
    nb = S // L
    row = lambda c: (lambda b, i: (b * nb + i, c))
    const = lambda b, i: (0, 0)
    return pl.pallas_call(
        _ssd_kernel,
        grid=(B, nb),
        in_specs=[
            pl.BlockSpec((L, SSD_CONV_DIM), row(COL_XBC // SSD_CONV_DIM)),
            pl.BlockSpec((L, 512), row(COL_GB // 512)),
            pl.BlockSpec((L, LANE), row(COL_SMALL // LANE)),
            pl.BlockSpec((SSD_CONV, SSD_CONV_DIM), const),
            pl.BlockSpec((1, SSD_CONV_DIM), const),
            pl.BlockSpec((1, LANE), const),
            pl.BlockSpec((1, LANE), const),
            pl.BlockSpec((1, SSD_DINNER), const),
            pl.BlockSpec((1, SSD_DINNER), const),
        ],
        out_specs=pl.BlockSpec((L, SSD_DINNER), row(0)),
        out_shape=jax.ShapeDtypeStruct((B * S, SSD_DINNER), BF16),
        scratch_shapes=[pltpu.VMEM((L + 8, SSD_CONV_DIM), F32),
                        pltpu.VMEM((SSD_H, SSD_HD, SSD_N), F32),
                        pltpu.VMEM((L, SSD_DINNER), F32)],
        compiler_params=_cparams(("parallel", "arbitrary")),
        name="ssd",
    )(u, u, u, cw, cb, dtb, a_pad, d_exp, ng)


def _rope64(x, cm, s1, s2):
    return x * cm + pltpu.roll(x, LANE - MLA_ROPE // 2, 1) * s1 + pltpu.roll(x, MLA_ROPE // 2, 1) * s2


def _mla_prep_kernel(c_ref, sm_ref, cm_ref, s1_ref, s2_ref, wq_ref, wkv_ref, qng_ref, kvng_ref,
                     qhg_ref, khg_ref, q_ref, k_ref, v_ref):
    blk = c_ref[...]
    cq = blk[:, :MLA_Q_LORA]
    ckv = blk[:, MLA_Q_LORA:]
    cqn = cq * lax.rsqrt(jnp.mean(cq * cq, axis=-1, keepdims=True) + EPS) * qng_ref[...]
    ckvn = ckv * lax.rsqrt(jnp.mean(ckv * ckv, axis=-1, keepdims=True) + EPS) * kvng_ref[...]
    qf = _dot(cqn.astype(BF16), wq_ref[...])
    kvf = _dot(ckvn.astype(BF16), wkv_ref[...])
    cm, s1, s2 = cm_ref[...], s1_ref[...], s2_ref[...]
    lane = lax.broadcasted_iota(jnp.int32, (1, LANE), 1)
    kr = jnp.where(lane < MLA_ROPE, sm_ref[...], 0.0)
    kr_ss = jnp.sum(kr * kr, axis=-1, keepdims=True)
    qhg, khg = qhg_ref[...], khg_ref[...]
    for h in range(MLA_H):
        qn = qf[:, 256 * h:256 * h + LANE]
        qr = qf[:, 256 * h + LANE:256 * (h + 1)]
        rq = lax.rsqrt((jnp.sum(qn * qn, axis=-1, keepdims=True)
                        + jnp.sum(qr * qr, axis=-1, keepdims=True)) * (1.0 / MLA_QK) + EPS)
        q_ref[0, h, :, 0:LANE] = (qn * rq * qhg[:, :LANE]).astype(BF16)
        q_ref[0, h, :, LANE:2 * LANE] = _rope64(qr * rq * qhg[:, LANE:], cm, s1, s2).astype(BF16)
        kn = kvf[:, 256 * h:256 * h + LANE]
        rk = lax.rsqrt((jnp.sum(kn * kn, axis=-1, keepdims=True) + kr_ss) * (1.0 / MLA_QK) + EPS)
        k_ref[0, h, :, 0:LANE] = (kn * rk * khg[:, :LANE]).astype(BF16)
        k_ref[0, h, :, LANE:2 * LANE] = _rope64(kr * rk * khg[:, LANE:], cm, s1, s2).astype(BF16)
        v_ref[0, h, :, :] = kvf[:, 256 * h + LANE:256 * (h + 1)].astype(BF16)


def _mla_prep(u, cm, s1, s2, wq, wkv, qng, kvng, qhg, khg, B, S, ts=512):
    ts = min(ts, S)
    nb = S // ts
    row = lambda c: (lambda b, i: (b * nb + i, c))
    const = lambda b, i: (0, 0)
    hspec = lambda w: pl.BlockSpec((1, MLA_H, ts, w), lambda b, i: (b, 0, i, 0))
    return pl.pallas_call(
        _mla_prep_kernel,
        grid=(B, nb),
        in_specs=[
            pl.BlockSpec((ts, 512), row(COL_CQKV // 512)),
            pl.BlockSpec((ts, LANE), row(COL_SMALL // LANE)),
            pl.BlockSpec((ts, LANE), row(0)), pl.BlockSpec((ts, LANE), row(0)), pl.BlockSpec((ts, LANE), row(0)),
            pl.BlockSpec((MLA_Q_LORA, MLA_H * 256), const),
            pl.BlockSpec((MLA_KV_LORA, MLA_H * 256), const),
            pl.BlockSpec((1, MLA_Q_LORA), const),
            pl.BlockSpec((1, MLA_KV_LORA), const),
            pl.BlockSpec((1, 256), const),
            pl.BlockSpec((1, 256), const),
        ],
        out_specs=[hspec(256), hspec(256), hspec(MLA_V)],
        out_shape=[jax.ShapeDtypeStruct((B, MLA_H, S, 256), BF16),
                   jax.ShapeDtypeStruct((B, MLA_H, S, 256), BF16),
                   jax.ShapeDtypeStruct((B, MLA_H, S, MLA_V), BF16)],
        compiler_params=_cparams(("parallel", "parallel")),
        name="mla_prep",
    )(u, u, cm, s1, s2, wq, wkv, qng, kvng, qhg, khg)


def _flash_kernel(q_ref, k_ref, v_ref, g_ref, o_ref, m_ref, l_ref, acc_ref, *, tq, tk):
    i, j = pl.program_id(2), pl.program_id(3)

    @pl.when(j == 0)
    def _():
        m_ref[...] = jnp.full_like(m_ref, -jnp.inf)
        l_ref[...] = jnp.zeros_like(l_ref)
        acc_ref[...] = jnp.zeros_like(acc_ref)

    @pl.when(j * tk <= i * tq + (tq - 1))
    def _():
        s = _dot_nt(q_ref[0, 0], k_ref[0, 0]) * (MLA_QK ** -0.5)
        qpos = i * tq + lax.broadcasted_iota(jnp.int32, (tq, tk), 0)
        kpos = j * tk + lax.broadcasted_iota(jnp.int32, (tq, tk), 1)
        s = jnp.where(kpos <= qpos, s, -jnp.inf)
        m_old = m_ref[...]
        m_new = jnp.maximum(m_old, jnp.max(s, axis=-1, keepdims=True))
        alpha = jnp.exp(m_old - m_new)
        p = jnp.exp(s - m_new)
        l_ref[...] = alpha * l_ref[...] + jnp.sum(p, axis=-1, keepdims=True)
        acc_ref[...] = alpha * acc_ref[...] + _dot(p.astype(BF16), v_ref[0, 0])
        m_ref[...] = m_new

    @pl.when(j == pl.num_programs(3) - 1)
    def _():
        o_ref[...] = (acc_ref[...] / l_ref[...] * _silu(g_ref[...])).astype(BF16)


def _flash(q, k, v, u, B, S, tq=512, tk=512):
    tq, tk = min(tq, S), min(tk, S)
    nq, nk = S // tq, S // tk
    last = lambda i: (i * tq + tq - 1) // tk
    return pl.pallas_call(
        functools.partial(_flash_kernel, tq=tq, tk=tk),
        grid=(B, MLA_H, nq, nk),
        in_specs=[
            pl.BlockSpec((1, 1, tq, 256), lambda b, h, i, j: (b, h, i, 0)),
            pl.BlockSpec((1, 1, tk, 256), lambda b, h, i, j: (b, h, jnp.minimum(j, last(i)), 0)),
            pl.BlockSpec((1, 1, tk, MLA_V), lambda b, h, i, j: (b, h, jnp.minimum(j, last(i)), 0)),
            pl.BlockSpec((tq, LANE), lambda b, h, i, j: (b * nq + i, COL_GC // LANE + h)),
        ],
        out_specs=pl.BlockSpec((tq, LANE), lambda b, h, i, j: (b * nq + i, h)),
        out_shape=jax.ShapeDtypeStruct((B * S, MLA_H * MLA_V), BF16),
        scratch_shapes=[pltpu.VMEM((tq, 1), F32), pltpu.VMEM((tq, 1), F32), pltpu.VMEM((tq, MLA_V), F32)],
        compiler_params=_cparams(("parallel", "parallel", "parallel", "arbitrary")),
        name="mla_flash",
    )(q, k, v, u)


def _dil_kernel(q_ref, k_ref, v_ref, g_ref, cd_ref, sd_ref, qg_ref, kg_ref, o_ref,
                qs_ref, ks_ref, vs_ref, acc_ref, m_ref, l_ref, *, S):
    cd, sd = cd_ref[...], sd_ref[...]

    def norm_rope(x, g):
        xn = x * lax.rsqrt(jnp.mean(x * x, axis=-1, keepdims=True) + EPS) * g
        return xn * cd + pltpu.roll(xn, DIL_HD // 2, 1) * sd

    qs_ref[...] = norm_rope(q_ref[...], qg_ref[...])
    ks_ref[...] = norm_rope(k_ref[...], kg_ref[...])
    vs_ref[...] = v_ref[...]
    m_ref[...] = jnp.full_like(m_ref, -jnp.inf)
    l_ref[...] = jnp.zeros_like(l_ref)
    acc_ref[...] = jnp.zeros_like(acc_ref)

    Q = DIL_BLK
    scale = DIL_HD ** -0.5
    ra = lax.broadcasted_iota(jnp.int32, (Q, 2 * Q), 0)
    cc = lax.broadcasted_iota(jnp.int32, (Q, 2 * Q), 1)
    band = (cc >= ra) & (cc <= ra + Q)

    for (w, d) in DIL_CONFIGS:
        assert w // d == Q
        nblk = S // (d * Q)

        def unit(n, carry, d=d, nblk=nblk):
            r = n // nblk
            i = n % nblk
            cur = pl.ds(i * (d * Q) + r, Q, stride=d)
            prv = pl.ds(jnp.maximum(i - 1, 0) * (d * Q) + r, Q, stride=d)
            q = qs_ref[cur, :].astype(BF16)
            kk = jnp.concatenate([ks_ref[prv, :], ks_ref[cur, :]], axis=0).astype(BF16)
            vv = jnp.concatenate([vs_ref[prv, :], vs_ref[cur, :]], axis=0).astype(BF16)
            s = _dot_nt(q, kk) * scale
            valid = band & ((cc >= Q) | (i > 0))
            s = jnp.where(valid, s, -jnp.inf)
            m_c = jnp.max(s, axis=-1, keepdims=True)
            p = jnp.exp(s - m_c)
            l_c = jnp.sum(p, axis=-1, keepdims=True)
            o_c = _dot(p.astype(BF16), vv)
            m_old = m_ref[cur, :]
            m_new = jnp.maximum(m_old, m_c)
            a_old = jnp.exp(m_old - m_new)
            a_c = jnp.exp(m_c - m_new)
            acc_ref[cur, :] = a_old * acc_ref[cur, :] + a_c * o_c
            l_ref[cur, :] = a_old * l_ref[cur, :] + a_c * l_c
            m_ref[cur, :] = m_new
            return carry

        lax.fori_loop(0, d * nblk, unit, 0)

    o_ref[...] = (acc_ref[...] / l_ref[...] * _silu(g_ref[...])).astype(BF16)


def _dil(u, cd, sd, qg, kg, B, S):
    col = lambda c: (lambda b, h: (b, c // LANE + h))
    const = lambda b, h: (0, 0)
    blk = lambda im: pl.BlockSpec((S, LANE), im)
    return pl.pallas_call(
        functools.partial(_dil_kernel, S=S),
        grid=(B, DIL_H),
        in_specs=[blk(col(COL_DQ)), blk(col(COL_DK)), blk(col(COL_DV)), blk(col(COL_GD)),
                  blk(lambda b, h: (b, 0)), blk(lambda b, h: (b, 0)),
                  pl.BlockSpec((1, DIL_HD), const), pl.BlockSpec((1, DIL_HD), const)],
        out_specs=blk(lambda b, h: (b, h)),
        out_shape=jax.ShapeDtypeStruct((B * S, DIL_H * DIL_HD), BF16),
        scratch_shapes=[pltpu.VMEM((S, DIL_HD), F32)] * 6,
        compiler_params=_cparams(("parallel", "parallel")),
        name="dilated",
    )(u, u, u, u, cd, sd, qg, kg)


def _pack_w_in(w):
    widths = (512, 512, 512, 512, 256, 256, 512, GLA_GATE_RANK, SSD_CONV_DIM, SSD_H,
              MLA_Q_LORA, MLA_KV_LORA, MLA_ROPE, 512, 512, 512)
    parts, start = [], 0
    for wd in widths:
        parts.append(w[:, start:start + wd])
        start += wd
    (g_a, g_b, g_c, g_d, a_q, a_k, a_v, a_lr, b_xbc, b_dt, c_q, c_kv, c_kr, d_q, d_k, d_v) = parts
    pad = jnp.zeros((w.shape[0], LANE - MLA_ROPE - GLA_GATE_RANK - SSD_H), w.dtype)
    packed = jnp.concatenate([g_a, g_b, g_c, g_d, a_q, a_k, a_v, b_xbc, c_q, c_kv, d_q, d_k, d_v,
                              c_kr, a_lr, b_dt, pad], axis=1)
    assert packed.shape[1] == N_PACK
    return packed.astype(BF16)


def _row(v, width=None, offset=0):
    v = v.astype(F32).reshape(1, -1)
    if width is None:
        return v
    return jnp.pad(v, ((0, 0), (offset, width - offset - v.shape[1])))


def _pad_heads(w, n_heads, real, padded):
    k = w.shape[0]
    w = w.reshape(k, n_heads, real)
    return jnp.pad(w, ((0, 0), (0, 0), (0, padded - real))).reshape(k, n_heads * padded)


def kernel(x, positions, ln_g, w_in, w_out, gla_gate_w2, gla_gate_b, gla_norm_g, ssd_conv_w, ssd_conv_b,
           ssd_dt_bias, ssd_A_log, ssd_D, ssd_norm_g, mla_q_norm_g, mla_kv_norm_g, mla_w_uq, mla_w_ukv,
           mla_q_head_g, mla_k_head_g, dil_q_g, dil_k_g):
    B, S, D = x.shape
    depth = w_in.shape[0]
    assert D == D_MODEL and S % SSD_CHUNK == 0 and S % GLA_ROWS == 0
    assert all(S % (d * DIL_BLK) == 0 for _, d in DIL_CONFIGS)
    T = B * S
    cm, s1, s2, cd, sd = _rope_tables(positions)
    xf = x.reshape(T, D)
    for l in range(depth):
        u = _inproj(xf, _row(ln_g[l]), _pack_w_in(w_in[l]))
        w2p = jnp.pad(gla_gate_w2[l], ((SM_LR, LANE - SM_LR - GLA_GATE_RANK), (0, 0))).astype(BF16)
        ya = _gla(u, w2p, _row(gla_gate_b[l]), _row(gla_norm_g[l]), B, S)
        yb = _ssd(u, ssd_conv_w[l].astype(F32), _row(ssd_conv_b[l]),
                  _row(ssd_dt_bias[l], LANE, SM_DT), _row(ssd_A_log[l], LANE, SM_DT),
                  _row(jnp.repeat(ssd_D[l], SSD_HD)), _row(ssd_norm_g[l]), B, S)
        q, k, v = _mla_prep(u, cm, s1, s2,
                            _pad_heads(mla_w_uq[l], MLA_H, MLA_QK, 256).astype(BF16),
                            mla_w_ukv[l].astype(BF16), _row(mla_q_norm_g[l]), _row(mla_kv_norm_g[l]),
                            _row(mla_q_head_g[l], 256), _row(mla_k_head_g[l], 256), B, S)
        yc = _flash(q, k, v, u, B, S)
        yd = _dil(u, cd, sd, _row(dil_q_g[l]), _row(dil_k_g[l]), B, S)
        xf = _outproj(xf, ya, yb, yc, yd, w_out[l].astype(BF16))
    return xf.reshape(B, S, D)
```

```python
import functools
import math

import jax
import jax.numpy as jnp
import numpy as np
from jax import lax
from jax.experimental import pallas as pl
from jax.experimental.pallas import tpu as pltpu

F32 = jnp.float32
BF16 = jnp.bfloat16

D_MODEL = 2048
GLA_H, GLA_DK, GLA_DV = 4, 64, 128
GLA_GATE_RANK = 16
GLA_GATE_TAU = 16.0
SSD_DINNER, SSD_HD, SSD_G, SSD_N, SSD_CONV = 512, 64, 2, 128, 4
SSD_H = SSD_DINNER // SSD_HD
SSD_CONV_DIM = SSD_DINNER + 2 * SSD_G * SSD_N
MLA_H, MLA_NOPE, MLA_ROPE, MLA_V = 4, 128, 64, 128
MLA_QK = MLA_NOPE + MLA_ROPE
MLA_Q_LORA, MLA_KV_LORA = 384, 128
DIL_H, DIL_HD = 4, 128
DIL_CONFIGS = ((128, 1), (512, 4), (2048, 16))
ROPE_THETA = 10000.0
EPS = 1e-6
LOG2E = math.log2(math.e)
D_MIX = GLA_H * GLA_DV + SSD_DINNER + MLA_H * MLA_V + DIL_H * DIL_HD

LANE = 128
VMEM_LIMIT = 52 * 1024 * 1024

COL_GA, COL_GB, COL_GC, COL_GD = 0, 512, 1024, 1536
COL_AQK = 2048
COL_AV = 2560
COL_XBC = 3072
COL_CQKV = 4096
COL_DQ, COL_DK, COL_DV = 4608, 5120, 5632
COL_SMALL = 6144
N_PACK = 6272
SM_KR, SM_LR, SM_DT = 0, 64, 80

GLA_CHUNK = 32
GLA_ROWS = 256
SSD_CHUNK = 256
DIL_BLK = 128
DIL_UNROLL = 4


def _cparams(sem):
    return pltpu.CompilerParams(dimension_semantics=sem, vmem_limit_bytes=VMEM_LIMIT)


def _silu(x):
    return x * (1.0 / (1.0 + jnp.exp(-x)))


def _dot(a, b):
    return jnp.dot(a, b, preferred_element_type=F32)


def _dot_nt(a, b):
    return lax.dot_general(a, b, (((1,), (1,)), ((), ())), preferred_element_type=F32)


def _dot_tn(a, b):
    return lax.dot_general(a, b, (((0,), (0,)), ((), ())), preferred_element_type=F32)


def _split_dot(tri, x):
    hi = x.astype(BF16)
    lo = (x - hi.astype(F32)).astype(BF16)
    return _dot(tri, hi) + _dot(tri, lo)


def _rope_tables_kernel(pos_ref, fm_ref, fd_ref, cm_ref, s1_ref, s2_ref, cd_ref, sd_ref):
    pos = pos_ref[...]
    lane = lax.broadcasted_iota(jnp.int32, (1, LANE), 1)
    ang_m = pos * fm_ref[...]
    cos_m, sin_m = jnp.cos(ang_m), jnp.sin(ang_m)
    cm_ref[...] = jnp.where(lane < MLA_ROPE, cos_m, 0.0)
    s1_ref[...] = jnp.where(lane < MLA_ROPE // 2, -sin_m, 0.0)
    s2_ref[...] = jnp.where((lane >= MLA_ROPE // 2) & (lane < MLA_ROPE), sin_m, 0.0)
    ang_d = pos * fd_ref[...]
    cd_ref[...] = jnp.cos(ang_d)
    sd_ref[...] = jnp.where(lane < DIL_HD // 2, -jnp.sin(ang_d), jnp.sin(ang_d))


def _rope_tables(positions):
    T = positions.size
    ts = min(T, 2048)
    pos = positions.reshape(T, 1).astype(F32)
    lane = np.arange(LANE)
    fm = np.exp(-math.log(ROPE_THETA) * (lane % (MLA_ROPE // 2)) * (2.0 / MLA_ROPE))
    fd = np.exp(-math.log(ROPE_THETA) * (lane % (DIL_HD // 2)) * (2.0 / DIL_HD))
    fm = jnp.asarray(fm, F32).reshape(1, LANE)
    fd = jnp.asarray(fd, F32).reshape(1, LANE)
    row = pl.BlockSpec((ts, LANE), lambda i: (i, 0))
    const = pl.BlockSpec((1, LANE), lambda i: (0, 0))
    return pl.pallas_call(
        _rope_tables_kernel,
        grid=(T // ts,),
        in_specs=[pl.BlockSpec((ts, 1), lambda i: (i, 0)), const, const],
        out_specs=[row] * 5,
        out_shape=[jax.ShapeDtypeStruct((T, LANE), F32)] * 5,
        compiler_params=_cparams(("parallel",)),
        name="rope_tables",
    )(pos, fm, fd)


def _inproj_kernel(x_ref, g_ref, w_ref, o_ref, h_ref):
    @pl.when(pl.program_id(1) == 0)
    def _():
        x = x_ref[...]
        ms = jnp.mean(x * x, axis=-1, keepdims=True)
        h_ref[...] = (x * lax.rsqrt(ms + EPS) * g_ref[...]).astype(BF16)

    o_ref[...] = _dot(h_ref[...], w_ref[...])


def _inproj(x, g, w, tm=1024, tn=896):
    T = x.shape[0]
    tm = min(tm, T)
    return pl.pallas_call(
        _inproj_kernel,
        grid=(T // tm, N_PACK // tn),
        in_specs=[
            pl.BlockSpec((tm, D_MODEL), lambda i, j: (i, 0)),
            pl.BlockSpec((1, D_MODEL), lambda i, j: (0, 0)),
            pl.BlockSpec((D_MODEL, tn), lambda i, j: (0, j)),
        ],
        out_specs=pl.BlockSpec((tm, tn), lambda i, j: (i, j)),
        out_shape=jax.ShapeDtypeStruct((T, N_PACK), F32),
        scratch_shapes=[pltpu.VMEM((tm, D_MODEL), BF16)],
        compiler_params=_cparams(("parallel", "arbitrary")),
        name="inproj",
    )(x, g, w)


def _outproj_kernel(x_ref, ya_ref, yb_ref, yc_ref, yd_ref, w_ref, o_ref):
    acc = x_ref[...]
    for n, y_ref in enumerate((ya_ref, yb_ref, yc_ref, yd_ref)):
        acc = acc + _dot(y_ref[...], w_ref[n * 512:(n + 1) * 512, :])
    o_ref[...] = acc


def _outproj(x, ya, yb, yc, yd, w, tm=256):
    T = x.shape[0]
    tm = min(tm, T)
    yspec = pl.BlockSpec((tm, 512), lambda i: (i, 0))
    return pl.pallas_call(
        _outproj_kernel,
        grid=(T // tm,),
        in_specs=[pl.BlockSpec((tm, D_MODEL), lambda i: (i, 0)), yspec, yspec, yspec, yspec,
                  pl.BlockSpec((D_MIX, D_MODEL), lambda i: (0, 0))],
        out_specs=pl.BlockSpec((tm, D_MODEL), lambda i: (i, 0)),
        out_shape=jax.ShapeDtypeStruct((T, D_MODEL), F32),
        compiler_params=_cparams(("parallel",)),
        name="outproj",
    )(x, ya, yb, yc, yd, w)


def _gla_kernel(qk_ref, v_ref, gate_ref, sm_ref, w2_ref, b2_ref, ng_ref, o_ref, st_ref):
    @pl.when(pl.program_id(1) == 0)
    def _():
        st_ref[...] = jnp.zeros_like(st_ref)

    R, C = GLA_ROWS, GLA_CHUNK
    xg = _dot(sm_ref[...].astype(BF16), w2_ref[...]) + b2_ref[...]
    logd = (jnp.minimum(xg, 0.0) - jnp.log(1.0 + jnp.exp(-jnp.abs(xg)))) * (1.0 / GLA_GATE_TAU)
    ri = lax.broadcasted_iota(jnp.int32, (R, R), 0)
    ci = lax.broadcasted_iota(jnp.int32, (R, R), 1)
    back = ri - ci
    in_chunk = (back >= 0) & (back <= (ri & (C - 1)))
    bc_all = _split_dot(jnp.where(in_chunk, 1.0, 0.0).astype(BF16), logd)
    ng = ng_ref[...]
    n_chunks = R // C
    hk = GLA_H * GLA_DK
    bc = bc_all.reshape(n_chunks, C, hk)
    b_mid = bc[:, C // 2 - 1:C // 2, :]
    b_last = bc[:, C - 1:C, :]
    q = (qk_ref[:, :hk] * (GLA_DK ** -0.5)).reshape(n_chunks, C, hk)
    k = qk_ref[:, hk:].reshape(n_chunks, C, hk)
    qd = (q * jnp.exp(bc - b_mid)).astype(BF16).reshape(R, hk)
    kd = (k * jnp.exp(b_mid - bc)).astype(BF16).reshape(R, hk)
    qe = (q * jnp.exp(bc)).astype(BF16).reshape(R, hk)
    kl = (k * jnp.exp(b_last - bc)).astype(BF16).reshape(R, hk)
    e_last = jnp.exp(b_last)
    wide = n_chunks * GLA_DK
    place = ((lax.broadcasted_iota(jnp.int32, (R, wide), 0) >> (C.bit_length() - 1))
             == (lax.broadcasted_iota(jnp.int32, (R, wide), 1) >> (GLA_DK.bit_length() - 1)))

    def block_diag(x):
        pair = jnp.concatenate([x, x], axis=1)
        return jnp.where(place, jnp.concatenate([pair] * (n_chunks // 2), axis=1), jnp.zeros((), x.dtype))

    for h in range(GLA_H):
        kcols = slice(h * GLA_DK, (h + 1) * GLA_DK)
        vcols = slice(h * GLA_DV, (h + 1) * GLA_DV)
        v = v_ref[:, vcols].astype(BF16)
        attn = jnp.where(in_chunk, _dot_nt(qd[:, kcols], kd[:, kcols]), 0.0).astype(BF16)
        u_all = _dot_tn(v, block_diag(kl[:, kcols]))
        st = st_ref[h]
        entering = []
        for c in range(n_chunks):
            entering.append(st)
            st = st * e_last[c][:, kcols] + u_all[:, c * GLA_DK:(c + 1) * GLA_DK]
        st_ref[h] = st
        s_all = jnp.concatenate(entering, axis=1).astype(BF16)
        o = _dot(attn, v) + _dot_nt(block_diag(qe[:, kcols]), s_all)
        ms = jnp.mean(o * o, axis=-1, keepdims=True)
        y = o * lax.rsqrt(ms + EPS) * ng * _silu(gate_ref[:, vcols])
        o_ref[:, vcols] = y.astype(BF16)


def _gla(u, w2p, b2, ng, B, S):
    R = GLA_ROWS
    nb = S // R
    row = lambda c: (lambda b, i: (b * nb + i, c))
    const = lambda b, i: (0, 0)
    return pl.pallas_call(
        _gla_kernel,
        grid=(B, nb),
        in_specs=[
            pl.BlockSpec((R, 512), row(COL_AQK // 512)),
            pl.BlockSpec((R, 512), row(COL_AV // 512)),
            pl.BlockSpec((R, 512), row(COL_GA // 512)),
            pl.BlockSpec((R, LANE), row(COL_SMALL // LANE)),
            pl.BlockSpec((LANE, GLA_H * GLA_DK), const),
            pl.BlockSpec((1, GLA_H * GLA_DK), const),
            pl.BlockSpec((1, GLA_DV), const),
        ],
        out_specs=pl.BlockSpec((R, 512), row(0)),
        out_shape=jax.ShapeDtypeStruct((B * S, GLA_H * GLA_DV), BF16),
        scratch_shapes=[pltpu.VMEM((GLA_H, GLA_DV, GLA_DK), F32)],
        compiler_params=_cparams(("parallel", "arbitrary")),
        name="gla",
    )(u, u, u, u, w2p, b2, ng)


def _ssd_kernel(xbc_ref, z_ref, sm_ref, cw_ref, cb_ref, dtb_ref, a_ref, d_ref, ng_ref, o_ref,
                xpad_ref, st_ref, y_ref):
    L = SSD_CHUNK
    HALO = 8

    @pl.when(pl.program_id(1) == 0)
    def _():
        st_ref[...] = jnp.zeros_like(st_ref)
        xpad_ref[0:HALO, :] = jnp.zeros((HALO, SSD_CONV_DIM), F32)

    xpad_ref[HALO:HALO + L, :] = xbc_ref[...]
    conv = cb_ref[...] + jnp.zeros((L, SSD_CONV_DIM), F32)
    for j in range(SSD_CONV):
        conv = conv + xpad_ref[pl.ds(HALO - (SSD_CONV - 1) + j, L), :] * cw_ref[j:j + 1, :]
    xpad_ref[0:HALO, :] = xpad_ref[L:L + HALO, :]
    xact = _silu(conv)

    lane = lax.broadcasted_iota(jnp.int32, (1, LANE), 1)
    is_dt = (lane >= SM_DT) & (lane < SM_DT + SSD_H)
    pre = sm_ref[...] + dtb_ref[...]
    dt_all = jnp.maximum(pre, 0.0) + jnp.log(1.0 + jnp.exp(-jnp.abs(pre)))
    a_all = jnp.where(is_dt, dt_all * -jnp.exp(a_ref[...]), 0.0)
    ri = lax.broadcasted_iota(jnp.int32, (L, L), 0)
    ci = lax.broadcasted_iota(jnp.int32, (L, L), 1)
    lower = ci <= ri
    cs_all = _split_dot(jnp.where(lower, 1.0, 0.0).astype(BF16), a_all)
    cs_t = cs_all.T

    heads_per_group = SSD_H // SSD_G
    for g in range(SSD_G):
        bm = xact[:, SSD_DINNER + g * SSD_N:SSD_DINNER + (g + 1) * SSD_N].astype(BF16)
        cm = xact[:, SSD_DINNER + SSD_G * SSD_N + g * SSD_N:
                  SSD_DINNER + SSD_G * SSD_N + (g + 1) * SSD_N].astype(BF16)
        scores = _dot_nt(cm, bm)
        for hh in range(heads_per_group):
            h = g * heads_per_group + hh
            xs = xact[:, h * SSD_HD:(h + 1) * SSD_HD]
            dt = dt_all[:, SM_DT + h:SM_DT + h + 1]
            cs_col = cs_all[:, SM_DT + h:SM_DT + h + 1]
            cs_row = cs_t[SM_DT + h:SM_DT + h + 1, :]
            cs_last = cs_col[L - 1:L, :]
            decay = jnp.where(lower, jnp.exp(cs_col - cs_row), 0.0)
            xdt = xs * dt
            y = _dot((scores * decay).astype(BF16), xdt.astype(BF16))
            prev = st_ref[h]
            y = y + _dot_nt(cm, prev.astype(BF16)) * jnp.exp(cs_col)
            st_ref[h] = prev * jnp.exp(cs_last) + _dot_tn(
                (xdt * jnp.exp(cs_last - cs_col)).astype(BF16), bm)
            y_ref[:, h * SSD_HD:(h + 1) * SSD_HD] = y + xs * d_ref[:, h * SSD_HD:(h + 1) * SSD_HD]

    yz = y_ref[...] * _silu(z_ref[...])
    gw = SSD_DINNER // SSD_G
    for g in range(SSD_G):
        yg = yz[:, g * gw:(g + 1) * gw]
        ms = jnp.mean(yg * yg, axis=-1, keepdims=True)
        o_ref[:, g * gw:(g + 1) * gw] = (yg * lax.rsqrt(ms + EPS) * ng_ref[:, g * gw:(g + 1) * gw]).astype(BF16)


def _ssd(u, cw, cb, dtb, a_pad, d_exp, ng, B, S):
    L = SSD_CHUNK
    nb = S // L
    row = lambda c: (lambda b, i: (b * nb + i, c))
    const = lambda b, i: (0, 0)
    return pl.pallas_call(
        _ssd_kernel,
        grid=(B, nb),
        in_specs=[
            pl.BlockSpec((L, SSD_CONV_DIM), row(COL_XBC // SSD_CONV_DIM)),
            pl.BlockSpec((L, 512), row(COL_GB // 512)),
            pl.BlockSpec((L, LANE), row(COL_SMALL // LANE)),
            pl.BlockSpec((SSD_CONV, SSD_CONV_DIM), const),
            pl.BlockSpec((1, SSD_CONV_DIM), const),
            pl.BlockSpec((1, LANE), const),
            pl.BlockSpec((1, LANE), const),
            pl.BlockSpec((1, SSD_DINNER), const),
            pl.BlockSpec((1, SSD_DINNER), const),
        ],
        out_specs=pl.BlockSpec((L, SSD_DINNER), row(0)),
        out_shape=jax.ShapeDtypeStruct((B * S, SSD_DINNER), BF16),
        scratch_shapes=[pltpu.VMEM((L + 8, SSD_CONV_DIM), F32),
                        pltpu.VMEM((SSD_H, SSD_HD, SSD_N), F32),
                        pltpu.VMEM((L, SSD_DINNER), F32)],
        compiler_params=_cparams(("parallel", "arbitrary")),
        name="ssd",
    )(u, u, u, cw, cb, dtb, a_pad, d_exp, ng)


def _rope64(x, cm, s1, s2):
    return x * cm + pltpu.roll(x, LANE - MLA_ROPE // 2, 1) * s1 + pltpu.roll(x, MLA_ROPE // 2, 1) * s2


def _mla_prep_kernel(c_ref, sm_ref, cm_ref, s1_ref, s2_ref, wq_ref, wkv_ref, qng_ref, kvng_ref,
                     qhg_ref, khg_ref, q_ref, k_ref, v_ref):
    blk = c_ref[...]
    cq = blk[:, :MLA_Q_LORA]
    ckv = blk[:, MLA_Q_LORA:]
    cqn = cq * lax.rsqrt(jnp.mean(cq * cq, axis=-1, keepdims=True) + EPS) * qng_ref[...]
    ckvn = ckv * lax.rsqrt(jnp.mean(ckv * ckv, axis=-1, keepdims=True) + EPS) * kvng_ref[...]
    qf = _dot(cqn.astype(BF16), wq_ref[...])
    kvf = _dot(ckvn.astype(BF16), wkv_ref[...])
    cm, s1, s2 = cm_ref[...], s1_ref[...], s2_ref[...]
    lane = lax.broadcasted_iota(jnp.int32, (1, LANE), 1)
    kr = jnp.where(lane < MLA_ROPE, sm_ref[...], 0.0)
    kr_ss = jnp.sum(kr * kr, axis=-1, keepdims=True)
    qhg, khg = qhg_ref[...], khg_ref[...]
    for h in range(MLA_H):
        qn = qf[:, 256 * h:256 * h + LANE]
        qr = qf[:, 256 * h + LANE:256 * (h + 1)]
        rq = lax.rsqrt((jnp.sum(qn * qn, axis=-1, keepdims=True)
                        + jnp.sum(qr * qr, axis=-1, keepdims=True)) * (1.0 / MLA_QK) + EPS)
        rq = rq * (MLA_QK ** -0.5 * LOG2E)
        q_ref[0, h, :, 0:LANE] = (qn * rq * qhg[:, :LANE]).astype(BF16)
        q_ref[0, h, :, LANE:2 * LANE] = _rope64(qr * rq * qhg[:, LANE:], cm, s1, s2).astype(BF16)
        kn = kvf[:, 256 * h:256 * h + LANE]
        rk = lax.rsqrt((jnp.sum(kn * kn, axis=-1, keepdims=True) + kr_ss) * (1.0 / MLA_QK) + EPS)
        k_ref[0, h, :, 0:LANE] = (kn * rk * khg[:, :LANE]).astype(BF16)
        k_ref[0, h, :, LANE:2 * LANE] = _rope64(kr * rk * khg[:, LANE:], cm, s1, s2).astype(BF16)
        v_ref[0, h, :, :] = kvf[:, 256 * h + LANE:256 * (h + 1)].astype(BF16)


def _mla_prep(u, cm, s1, s2, wq, wkv, qng, kvng, qhg, khg, B, S, ts=512):
    ts = min(ts, S)
    nb = S // ts
    row = lambda c: (lambda b, i: (b * nb + i, c))
    const = lambda b, i: (0, 0)
    hspec = lambda w: pl.BlockSpec((1, MLA_H, ts, w), lambda b, i: (b, 0, i, 0))
    return pl.pallas_call(
        _mla_prep_kernel,
        grid=(B, nb),
        in_specs=[
            pl.BlockSpec((ts, 512), row(COL_CQKV // 512)),
            pl.BlockSpec((ts, LANE), row(COL_SMALL // LANE)),
            pl.BlockSpec((ts, LANE), row(0)), pl.BlockSpec((ts, LANE), row(0)), pl.BlockSpec((ts, LANE), row(0)),
            pl.BlockSpec((MLA_Q_LORA, MLA_H * 256), const),
            pl.BlockSpec((MLA_KV_LORA, MLA_H * 256), const),
            pl.BlockSpec((1, MLA_Q_LORA), const),
            pl.BlockSpec((1, MLA_KV_LORA), const),
            pl.BlockSpec((1, 256), const),
            pl.BlockSpec((1, 256), const),
        ],
        out_specs=[hspec(256), hspec(256), hspec(MLA_V)],
        out_shape=[jax.ShapeDtypeStruct((B, MLA_H, S, 256), BF16),
                   jax.ShapeDtypeStruct((B, MLA_H, S, 256), BF16),
                   jax.ShapeDtypeStruct((B, MLA_H, S, MLA_V), BF16)],
        compiler_params=_cparams(("parallel", "parallel")),
        name="mla_prep",
    )(u, u, cm, s1, s2, wq, wkv, qng, kvng, qhg, khg)


def _flash_kernel(q_ref, k_ref, v_ref, g_ref, o_ref, m_ref, l_ref, acc_ref, *, t):
    i = pl.program_id(2)
    m_ref[...] = jnp.full_like(m_ref, -jnp.inf)
    l_ref[...] = jnp.zeros_like(l_ref)
    acc_ref[...] = jnp.zeros_like(acc_ref)
    q = q_ref[0, 0]
    causal = lax.broadcasted_iota(jnp.int32, (t, t), 1) <= lax.broadcasted_iota(jnp.int32, (t, t), 0)

    def block(j, diagonal):
        rows = pl.ds(pl.multiple_of(j * t, t), t)
        s = _dot_nt(q, k_ref[0, 0, rows, :])
        if diagonal:
            s = jnp.where(causal, s, -jnp.inf)
        m_old = m_ref[...]
        m_new = jnp.maximum(m_old, jnp.max(s, axis=-1, keepdims=True))
        alpha = jnp.exp2(m_old - m_new)
        p = jnp.exp2(s - jnp.concatenate([m_new] * (t // LANE), axis=1))
        l_ref[...] = alpha * l_ref[...] + jnp.sum(p, axis=-1, keepdims=True)
        acc_ref[...] = alpha * acc_ref[...] + _dot(p.astype(BF16), v_ref[0, 0, rows, :])
        m_ref[...] = m_new

    def body(j, carry):
        block(j, False)
        return carry

    lax.fori_loop(0, i, body, 0)
    block(i, True)
    o_ref[...] = (acc_ref[...] / l_ref[...] * _silu(g_ref[...])).astype(BF16)


def _flash(q, k, v, u, B, S, t=512):
    t = min(t, S)
    nq = S // t
    return pl.pallas_call(
        functools.partial(_flash_kernel, t=t),
        grid=(B, MLA_H, nq),
        in_specs=[
            pl.BlockSpec((1, 1, t, 256), lambda b, h, i: (b, h, i, 0)),
            pl.BlockSpec((1, 1, S, 256), lambda b, h, i: (b, h, 0, 0)),
            pl.BlockSpec((1, 1, S, MLA_V), lambda b, h, i: (b, h, 0, 0)),
            pl.BlockSpec((t, LANE), lambda b, h, i: (b * nq + i, COL_GC // LANE + h)),
        ],
        out_specs=pl.BlockSpec((t, LANE), lambda b, h, i: (b * nq + i, h)),
        out_shape=jax.ShapeDtypeStruct((B * S, MLA_H * MLA_V), BF16),
        scratch_shapes=[pltpu.VMEM((t, LANE), F32), pltpu.VMEM((t, LANE), F32), pltpu.VMEM((t, MLA_V), F32)],
        compiler_params=_cparams(("parallel", "parallel", "arbitrary")),
        name="mla_flash",
    )(q, k, v, u)


def _dil_kernel(q_ref, k_ref, v_ref, g_ref, cd_ref, sd_ref, qg_ref, kg_ref, o_ref,
                qs_ref, ks_ref, qd_ref, kd_ref, vd_ref, oc_ref, ec_ref, *, S):
    Q = DIL_BLK
    cd, sd = cd_ref[...], sd_ref[...]

    def norm_rope(x, g):
        xn = x * lax.rsqrt(jnp.mean(x * x, axis=-1, keepdims=True) + EPS) * g
        return xn * cd + pltpu.roll(xn, DIL_HD // 2, 1) * sd

    qs_ref[...] = norm_rope(q_ref[...], qg_ref[...] * (DIL_HD ** -0.5 * LOG2E))
    ks_ref[...] = norm_rope(k_ref[...], kg_ref[...])

    ra = lax.broadcasted_iota(jnp.int32, (Q, 2 * Q), 0)
    cc = lax.broadcasted_iota(jnp.int32, (Q, 2 * Q), 1)
    band = (cc >= ra) & (cc <= ra + Q)
    bias_inner = jnp.where(band, 0.0, -jnp.inf)
    bias_first = jnp.where(band & (cc >= Q), 0.0, -jnp.inf)
    n_units = S // Q

    for c, (w, d) in enumerate(DIL_CONFIGS):
        assert w // d == Q
        n_sub = S // d
        pitch = n_sub + Q
        nblk = n_sub // Q
        piece = min(n_sub, 512)
        for r in range(d):
            kd_ref[r * pitch:r * pitch + Q, :] = jnp.zeros((Q, DIL_HD), BF16)
            vd_ref[r * pitch:r * pitch + Q, :] = jnp.zeros((Q, DIL_HD), BF16)
            for c0 in range(0, n_sub, piece):
                src = pl.ds(r + c0 * d, piece, stride=d)
                qd_ref[r * n_sub + c0:r * n_sub + c0 + piece, :] = qs_ref[src, :].astype(BF16)
                kd_ref[r * pitch + Q + c0:r * pitch + Q + c0 + piece, :] = ks_ref[src, :].astype(BF16)
                vd_ref[r * pitch + Q + c0:r * pitch + Q + c0 + piece, :] = v_ref[src, :].astype(BF16)

        unroll = min(DIL_UNROLL, n_units)

        def group(n0, carry, c=c, n_sub=n_sub, pitch=pitch, nblk=nblk, unroll=unroll):
            for uu in range(unroll):
                n = n0 * unroll + uu
                r = n // nblk
                i = n - r * nblk
                qrows = pl.ds(pl.multiple_of(r * n_sub + i * Q, Q), Q)
                krows = pl.ds(pl.multiple_of(r * pitch + i * Q, Q), 2 * Q)
                s = _dot_nt(qd_ref[qrows, :], kd_ref[krows, :]) + jnp.where(i > 0, bias_inner, bias_first)
                m = jnp.max(s, axis=-1, keepdims=True)
                p = jnp.exp2(s - m)
                l = jnp.sum(p, axis=-1, keepdims=True)
                oc_ref[c, qrows, :] = _dot(p.astype(BF16), vd_ref[krows, :]) * (1.0 / l)
                ec_ref[c, qrows, :] = jnp.broadcast_to(m + jnp.log2(l), (Q, LANE))
            return carry

        lax.fori_loop(0, n_units // unroll, group, 0)

    d_max = DIL_CONFIGS[-1][1]
    nblk_max = S // (d_max * Q)
    unroll = min(DIL_UNROLL, n_units)

    def merge(n0, carry):
        for uu in range(unroll):
            n = n0 * unroll + uu
            r = n // nblk_max
            i = n - r * nblk_max
            nat = pl.ds(i * (d_max * Q) + r, Q, stride=d_max)
            os, es = [], []
            for c, (w, d) in enumerate(DIL_CONFIGS):
                step = d_max // d
                start = (r % d) * (S // d) + i * (step * Q) + r // d
                rows = pl.ds(start, Q, stride=step) if step > 1 else pl.ds(start, Q)
                os.append(oc_ref[c, rows, :])
                es.append(ec_ref[c, rows, :])
            e_max = functools.reduce(jnp.maximum, es)
            ws = [jnp.exp2(e - e_max) for e in es]
            num = sum(wt * o for wt, o in zip(ws, os))
            qs_ref[nat, :] = num / sum(ws) * _silu(g_ref[nat, :])
        return carry

    lax.fori_loop(0, n_units // unroll, merge, 0)
    o_ref[...] = qs_ref[...].astype(BF16)


def _dil(u, cd, sd, qg, kg, B, S):
    col = lambda c: (lambda b, h: (b, c // LANE + h))
    const = lambda b, h: (0, 0)
    blk = lambda im: pl.BlockSpec((S, LANE), im)
    kv_rows = max(S + d * DIL_BLK for _, d in DIL_CONFIGS)
    return pl.pallas_call(
        functools.partial(_dil_kernel, S=S),
        grid=(B, DIL_H),
        in_specs=[blk(col(COL_DQ)), blk(col(COL_DK)), blk(col(COL_DV)), blk(col(COL_GD)),
                  blk(lambda b, h: (b, 0)), blk(lambda b, h: (b, 0)),
                  pl.BlockSpec((1, DIL_HD), const), pl.BlockSpec((1, DIL_HD), const)],
        out_specs=blk(lambda b, h: (b, h)),
        out_shape=jax.ShapeDtypeStruct((B * S, DIL_H * DIL_HD), BF16),
        scratch_shapes=[pltpu.VMEM((S, DIL_HD), F32), pltpu.VMEM((S, DIL_HD), F32),
                        pltpu.VMEM((S, DIL_HD), BF16),
                        pltpu.VMEM((kv_rows, DIL_HD), BF16), pltpu.VMEM((kv_rows, DIL_HD), BF16),
                        pltpu.VMEM((len(DIL_CONFIGS), S, DIL_HD), F32),
                        pltpu.VMEM((len(DIL_CONFIGS), S, LANE), F32)],
        compiler_params=_cparams(("parallel", "parallel")),
        name="dilated",
    )(u, u, u, u, cd, sd, qg, kg)


def _pack_w_in(w):
    widths = (512, 512, 512, 512, 256, 256, 512, GLA_GATE_RANK, SSD_CONV_DIM, SSD_H,
              MLA_Q_LORA, MLA_KV_LORA, MLA_ROPE, 512, 512, 512)
    parts, start = [], 0
    for wd in widths:
        parts.append(w[:, start:start + wd])
        start += wd
    (g_a, g_b, g_c, g_d, a_q, a_k, a_v, a_lr, b_xbc, b_dt, c_q, c_kv, c_kr, d_q, d_k, d_v) = parts
    pad = jnp.zeros((w.shape[0], LANE - MLA_ROPE - GLA_GATE_RANK - SSD_H), w.dtype)
    packed = jnp.concatenate([g_a, g_b, g_c, g_d, a_q, a_k, a_v, b_xbc, c_q, c_kv, d_q, d_k, d_v,
                              c_kr, a_lr, b_dt, pad], axis=1)
    assert packed.shape[1] == N_PACK
    return packed.astype(BF16)


def _row(v, width=None, offset=0):
    v = v.astype(F32).reshape(1, -1)
    if width is None:
        return v
    return jnp.pad(v, ((0, 0), (offset, width - offset - v.shape[1])))


def _pad_heads(w, n_heads, real, padded):
    k = w.shape[0]
    w = w.reshape(k, n_heads, real)
    return jnp.pad(w, ((0, 0), (0, 0), (0, padded - real))).reshape(k, n_heads * padded)


def kernel(x, positions, ln_g, w_in, w_out, gla_gate_w2, gla_gate_b, gla_norm_g, ssd_conv_w, ssd_conv_b,
           ssd_dt_bias, ssd_A_log, ssd_D, ssd_norm_g, mla_q_norm_g, mla_kv_norm_g, mla_w_uq, mla_w_ukv,
           mla_q_head_g, mla_k_head_g, dil_q_g, dil_k_g):
    B, S, D = x.shape
    depth = w_in.shape[0]
    assert D == D_MODEL and S % SSD_CHUNK == 0 and S % GLA_ROWS == 0
    assert all(S % (d * DIL_BLK) == 0 for _, d in DIL_CONFIGS)
    T = B * S
    cm, s1, s2, cd, sd = _rope_tables(positions)
    xf = x.reshape(T, D)
    for l in range(depth):
        u = _inproj(xf, _row(ln_g[l]), _pack_w_in(w_in[l]))
        w2p = jnp.pad(gla_gate_w2[l], ((SM_LR, LANE - SM_LR - GLA_GATE_RANK), (0, 0))).astype(BF16)
        ya = _gla(u, w2p, _row(gla_gate_b[l]), _row(gla_norm_g[l]), B, S)
        yb = _ssd(u, ssd_conv_w[l].astype(F32), _row(ssd_conv_b[l]),
                  _row(ssd_dt_bias[l], LANE, SM_DT), _row(ssd_A_log[l], LANE, SM_DT),
                  _row(jnp.repeat(ssd_D[l], SSD_HD)), _row(ssd_norm_g[l]), B, S)
        q, k, v = _mla_prep(u, cm, s1, s2,
                            _pad_heads(mla_w_uq[l], MLA_H, MLA_QK, 256).astype(BF16),
                            mla_w_ukv[l].astype(BF16), _row(mla_q_norm_g[l]), _row(mla_kv_norm_g[l]),
                            _row(mla_q_head_g[l], 256), _row(mla_k_head_g[l], 256), B, S)
        yc = _flash(q, k, v, u, B, S)
        yd = _dil(u, cd, sd, _row(dil_q_g[l]), _row(dil_k_g[l]), B, S)
        xf = _outproj(xf, ya, yb, yc, yd, w_out[l].astype(BF16))
    return xf.reshape(B, S, D)
```

```python
import functools
import math

import jax
import jax.numpy as jnp
import numpy as np
from jax import lax
from jax.experimental import pallas as pl
from jax.experimental.pallas import tpu as pltpu

F32 = jnp.float32
BF16 = jnp.bfloat16

D_MODEL = 2048
GLA_H, GLA_DK, GLA_DV = 4, 64, 128
GLA_GATE_RANK = 16
GLA_GATE_TAU = 16.0
SSD_DINNER, SSD_HD, SSD_G, SSD_N, SSD_CONV = 512, 64, 2, 128, 4
SSD_H = SSD_DINNER // SSD_HD
SSD_CONV_DIM = SSD_DINNER + 2 * SSD_G * SSD_N
MLA_H, MLA_NOPE, MLA_ROPE, MLA_V = 4, 128, 64, 128
MLA_QK = MLA_NOPE + MLA_ROPE
MLA_Q_LORA, MLA_KV_LORA = 384, 128
DIL_H, DIL_HD = 4, 128
DIL_CONFIGS = ((128, 1), (512, 4), (2048, 16))
ROPE_THETA = 10000.0
EPS = 1e-6
LOG2E = math.log2(math.e)
D_MIX = GLA_H * GLA_DV + SSD_DINNER + MLA_H * MLA_V + DIL_H * DIL_HD

LANE = 128
VMEM_LIMIT = 52 * 1024 * 1024

COL_GA, COL_GB, COL_GC, COL_GD = 0, 512, 1024, 1536
COL_AQK = 2048
COL_AV = 2560
COL_XBC = 3072
COL_CQKV = 4096
COL_DQ, COL_DK, COL_DV = 4608, 5120, 5632
COL_SMALL = 6144
N_PACK = 6272
SM_KR, SM_LR, SM_DT = 0, 64, 80

GLA_CHUNK = 32
GLA_ROWS = 256
SSD_CHUNK = 256
DIL_BLK = 128
DIL_UNROLL = 16
DIL_DIRECT_STRIDE = 4


def _cparams(sem):
    return pltpu.CompilerParams(dimension_semantics=sem, vmem_limit_bytes=VMEM_LIMIT)


def _silu(x):
    return x * (1.0 / (1.0 + jnp.exp(-x)))


def _dot(a, b):
    return jnp.dot(a, b, preferred_element_type=F32)


def _dot_nt(a, b):
    return lax.dot_general(a, b, (((1,), (1,)), ((), ())), preferred_element_type=F32)


def _dot_tn(a, b):
    return lax.dot_general(a, b, (((0,), (0,)), ((), ())), preferred_element_type=F32)


def _split_dot(tri, x):
    hi = x.astype(BF16)
    lo = (x - hi.astype(F32)).astype(BF16)
    return _dot(tri, hi) + _dot(tri, lo)


def _rope_tables_kernel(pos_ref, fm_ref, fd_ref, cm_ref, s1_ref, s2_ref, cd_ref, sd_ref):
    pos = pos_ref[...]
    lane = lax.broadcasted_iota(jnp.int32, (1, LANE), 1)
    ang_m = pos * fm_ref[...]
    cos_m, sin_m = jnp.cos(ang_m), jnp.sin(ang_m)
    cm_ref[...] = jnp.where(lane < MLA_ROPE, cos_m, 0.0)
    s1_ref[...] = jnp.where(lane < MLA_ROPE // 2, -sin_m, 0.0)
    s2_ref[...] = jnp.where((lane >= MLA_ROPE // 2) & (lane < MLA_ROPE), sin_m, 0.0)
    ang_d = pos * fd_ref[...]
    cd_ref[...] = jnp.cos(ang_d)
    sd_ref[...] = jnp.where(lane < DIL_HD // 2, -jnp.sin(ang_d), jnp.sin(ang_d))


def _rope_tables(positions):
    T = positions.size
    ts = min(T, 2048)
    pos = positions.reshape(T, 1).astype(F32)
    lane = np.arange(LANE)
    fm = np.exp(-math.log(ROPE_THETA) * (lane % (MLA_ROPE // 2)) * (2.0 / MLA_ROPE))
    fd = np.exp(-math.log(ROPE_THETA) * (lane % (DIL_HD // 2)) * (2.0 / DIL_HD))
    fm = jnp.asarray(fm, F32).reshape(1, LANE)
    fd = jnp.asarray(fd, F32).reshape(1, LANE)
    row = pl.BlockSpec((ts, LANE), lambda i: (i, 0))
    const = pl.BlockSpec((1, LANE), lambda i: (0, 0))
    return pl.pallas_call(
        _rope_tables_kernel,
        grid=(T // ts,),
        in_specs=[pl.BlockSpec((ts, 1), lambda i: (i, 0)), const, const],
        out_specs=[row] * 5,
        out_shape=[jax.ShapeDtypeStruct((T, LANE), F32)] * 5,
        compiler_params=_cparams(("parallel",)),
        name="rope_tables",
    )(pos, fm, fd)


def _inproj_kernel(x_ref, g_ref, w_ref, o_ref, h_ref):
    @pl.when(pl.program_id(1) == 0)
    def _():
        x = x_ref[...]
        ms = jnp.mean(x * x, axis=-1, keepdims=True)
        h_ref[...] = (x * lax.rsqrt(ms + EPS) * g_ref[...]).astype(BF16)

    o_ref[...] = _dot(h_ref[...], w_ref[...])


def _inproj(x, g, w, layer, tm=1024, tn=896):
    T = x.shape[0]
    tm = min(tm, T)
    return pl.pallas_call(
        _inproj_kernel,
        grid=(T // tm, N_PACK // tn),
        in_specs=[
            pl.BlockSpec((tm, D_MODEL), lambda i, j: (i, 0)),
            pl.BlockSpec((1, D_MODEL), lambda i, j: (0, 0)),
            pl.BlockSpec((None, D_MODEL, tn), lambda i, j: (layer, 0, j)),
        ],
        out_specs=pl.BlockSpec((tm, tn), lambda i, j: (i, j)),
        out_shape=jax.ShapeDtypeStruct((T, N_PACK), F32),
        scratch_shapes=[pltpu.VMEM((tm, D_MODEL), BF16)],
        compiler_params=_cparams(("parallel", "arbitrary")),
        name="inproj",
    )(x, g, w)


def _outproj_kernel(x_ref, ya_ref, yb_ref, yc_ref, yd_ref, w_ref, o_ref):
    acc = x_ref[...]
    for n, y_ref in enumerate((ya_ref, yb_ref, yc_ref, yd_ref)):
        acc = acc + _dot(y_ref[...], w_ref[n * 512:(n + 1) * 512, :])
    o_ref[...] = acc


def _outproj(x, ya, yb, yc, yd, w, tm=256):
    T = x.shape[0]
    tm = min(tm, T)
    yspec = pl.BlockSpec((tm, 512), lambda i: (i, 0))
    return pl.pallas_call(
        _outproj_kernel,
        grid=(T // tm,),
        in_specs=[pl.BlockSpec((tm, D_MODEL), lambda i: (i, 0)), yspec, yspec, yspec, yspec,
                  pl.BlockSpec((D_MIX, D_MODEL), lambda i: (0, 0))],
        out_specs=pl.BlockSpec((tm, D_MODEL), lambda i: (i, 0)),
        out_shape=jax.ShapeDtypeStruct((T, D_MODEL), F32),
        compiler_params=_cparams(("parallel",)),
        name="outproj",
    )(x, ya, yb, yc, yd, w)


def _gla_kernel(qk_ref, v_ref, gate_ref, sm_ref, w2_ref, b2_ref, ng_ref, o_ref, st_ref):
    @pl.when(pl.program_id(1) == 0)
    def _():
        st_ref[...] = jnp.zeros_like(st_ref)

    R, C = GLA_ROWS, GLA_CHUNK
    xg = _dot(sm_ref[...].astype(BF16), w2_ref[...]) + b2_ref[...]
    logd = (jnp.minimum(xg, 0.0) - jnp.log(1.0 + jnp.exp(-jnp.abs(xg)))) * (1.0 / GLA_GATE_TAU)
    ri = lax.broadcasted_iota(jnp.int32, (R, R), 0)
    ci = lax.broadcasted_iota(jnp.int32, (R, R), 1)
    back = ri - ci
    in_chunk = (back >= 0) & (back <= (ri & (C - 1)))
    bc_all = _split_dot(jnp.where(in_chunk, 1.0, 0.0).astype(BF16), logd)
    ng = ng_ref[...]
    n_chunks = R // C
    hk = GLA_H * GLA_DK
    bc = bc_all.reshape(n_chunks, C, hk)
    b_mid = bc[:, C // 2 - 1:C // 2, :]
    b_last = bc[:, C - 1:C, :]
    q = (qk_ref[:, :hk] * (GLA_DK ** -0.5)).reshape(n_chunks, C, hk)
    k = qk_ref[:, hk:].reshape(n_chunks, C, hk)
    qd = (q * jnp.exp(bc - b_mid)).astype(BF16).reshape(R, hk)
    kd = (k * jnp.exp(b_mid - bc)).astype(BF16).reshape(R, hk)
    qe = (q * jnp.exp(bc)).astype(BF16).reshape(R, hk)
    kl = (k * jnp.exp(b_last - bc)).astype(BF16).reshape(R, hk)
    e_last = jnp.exp(b_last)
    wide = n_chunks * GLA_DK
    place = ((lax.broadcasted_iota(jnp.int32, (R, wide), 0) >> (C.bit_length() - 1))
             == (lax.broadcasted_iota(jnp.int32, (R, wide), 1) >> (GLA_DK.bit_length() - 1)))

    def block_diag(x):
        pair = jnp.concatenate([x, x], axis=1)
        return jnp.where(place, jnp.concatenate([pair] * (n_chunks // 2), axis=1), jnp.zeros((), x.dtype))

    for h in range(GLA_H):
        kcols = slice(h * GLA_DK, (h + 1) * GLA_DK)
        vcols = slice(h * GLA_DV, (h + 1) * GLA_DV)
        v = v_ref[:, vcols].astype(BF16)
        attn = jnp.where(in_chunk, _dot_nt(qd[:, kcols], kd[:, kcols]), 0.0).astype(BF16)
        u_all = _dot_tn(v, block_diag(kl[:, kcols]))
        st = st_ref[h]
        entering = []
        for c in range(n_chunks):
            entering.append(st)
            st = st * e_last[c][:, kcols] + u_all[:, c * GLA_DK:(c + 1) * GLA_DK]
        st_ref[h] = st
        s_all = jnp.concatenate(entering, axis=1).astype(BF16)
        o = _dot(attn, v) + _dot_nt(block_diag(qe[:, kcols]), s_all)
        ms = jnp.mean(o * o, axis=-1, keepdims=True)
        y = o * lax.rsqrt(ms + EPS) * ng * _silu(gate_ref[:, vcols])
        o_ref[:, vcols] = y.astype(BF16)


def _gla(u, w2p, b2, ng, B, S):
    R = GLA_ROWS
    nb = S // R
    row = lambda c: (lambda b, i: (b * nb + i, c))
    const = lambda b, i: (0, 0)
    return pl.pallas_call(
        _gla_kernel,
        grid=(B, nb),
        in_specs=[
            pl.BlockSpec((R, 512), row(COL_AQK // 512)),
            pl.BlockSpec((R, 512), row(COL_AV // 512)),
            pl.BlockSpec((R, 512), row(COL_GA // 512)),
            pl.BlockSpec((R, LANE), row(COL_SMALL // LANE)),
            pl.BlockSpec((LANE, GLA_H * GLA_DK), const),
            pl.BlockSpec((1, GLA_H * GLA_DK), const),
            pl.BlockSpec((1, GLA_DV), const),
        ],
        out_specs=pl.BlockSpec((R, 512), row(0)),
        out_shape=jax.ShapeDtypeStruct((B * S, GLA_H * GLA_DV), BF16),
        scratch_shapes=[pltpu.VMEM((GLA_H, GLA_DV, GLA_DK), F32)],
        compiler_params=_cparams(("parallel", "arbitrary")),
        name="gla",
    )(u, u, u, u, w2p, b2, ng)


def _ssd_kernel(xbc_ref, z_ref, sm_ref, cw_ref, cb_ref, dtb_ref, a_ref, d_ref, ng_ref, o_ref,
                xpad_ref, st_ref, y_ref):
    L = SSD_CHUNK
    HALO = 8

    @pl.when(pl.program_id(1) == 0)
    def _():
        st_ref[...] = jnp.zeros_like(st_ref)
        xpad_ref[0:HALO, :] = jnp.zeros((HALO, SSD_CONV_DIM), F32)

    xpad_ref[HALO:HALO + L, :] = xbc_ref[...]
    conv = cb_ref[...] + jnp.zeros((L, SSD_CONV_DIM), F32)
    for j in range(SSD_CONV):
        conv = conv + xpad_ref[pl.ds(HALO - (SSD_CONV - 1) + j, L), :] * cw_ref[j:j + 1, :]
    xpad_ref[0:HALO, :] = xpad_ref[L:L + HALO, :]
    xact = _silu(conv)

    lane = lax.broadcasted_iota(jnp.int32, (1, LANE), 1)
    is_dt = (lane >= SM_DT) & (lane < SM_DT + SSD_H)
    pre = sm_ref[...] + dtb_ref[...]
    dt_all = jnp.maximum(pre, 0.0) + jnp.log(1.0 + jnp.exp(-jnp.abs(pre)))
    a_all = jnp.where(is_dt, dt_all * -jnp.exp(a_ref[...]), 0.0)
    ri = lax.broadcasted_iota(jnp.int32, (L, L), 0)
    ci = lax.broadcasted_iota(jnp.int32, (L, L), 1)
    lower = ci <= ri
    cs_all = _split_dot(jnp.where(lower, 1.0, 0.0).astype(BF16), a_all)
    cs_t = cs_all.T

    heads_per_group = SSD_H // SSD_G
    for g in range(SSD_G):
        bm = xact[:, SSD_DINNER + g * SSD_N:SSD_DINNER + (g + 1) * SSD_N].astype(BF16)
        cm = xact[:, SSD_DINNER + SSD_G * SSD_N + g * SSD_N:
                  SSD_DINNER + SSD_G * SSD_N + (g + 1) * SSD_N].astype(BF16)
        scores = _dot_nt(cm, bm)
        for hh in range(heads_per_group):
            h = g * heads_per_group + hh
            xs = xact[:, h * SSD_HD:(h + 1) * SSD_HD]
            dt = dt_all[:, SM_DT + h:SM_DT + h + 1]
            cs_col = cs_all[:, SM_DT + h:SM_DT + h + 1]
            cs_row = cs_t[SM_DT + h:SM_DT + h + 1, :]
            cs_last = cs_col[L - 1:L, :]
            decay = jnp.where(lower, jnp.exp(cs_col - cs_row), 0.0)
            xdt = xs * dt
            y = _dot((scores * decay).astype(BF16), xdt.astype(BF16))
            prev = st_ref[h]
            y = y + _dot_nt(cm, prev.astype(BF16)) * jnp.exp(cs_col)
            st_ref[h] = prev * jnp.exp(cs_last) + _dot_tn(
                (xdt * jnp.exp(cs_last - cs_col)).astype(BF16), bm)
            y_ref[:, h * SSD_HD:(h + 1) * SSD_HD] = y + xs * d_ref[:, h * SSD_HD:(h + 1) * SSD_HD]

    yz = y_ref[...] * _silu(z_ref[...])
    gw = SSD_DINNER // SSD_G
    for g in range(SSD_G):
        yg = yz[:, g * gw:(g + 1) * gw]
        ms = jnp.mean(yg * yg, axis=-1, keepdims=True)
        o_ref[:, g * gw:(g + 1) * gw] = (yg * lax.rsqrt(ms + EPS) * ng_ref[:, g * gw:(g + 1) * gw]).astype(BF16)


def _ssd(u, cw, cb, dtb, a_pad, d_exp, ng, B, S):
    L = SSD_CHUNK
    nb = S // L
    row = lambda c: (lambda b, i: (b * nb + i, c))
    const = lambda b, i: (0, 0)
    return pl.pallas_call(
        _ssd_kernel,
        grid=(B, nb),
        in_specs=[
            pl.BlockSpec((L, SSD_CONV_DIM), row(COL_XBC // SSD_CONV_DIM)),
            pl.BlockSpec((L, 512), row(COL_GB // 512)),
            pl.BlockSpec((L, LANE), row(COL_SMALL // LANE)),
            pl.BlockSpec((SSD_CONV, SSD_CONV_DIM), const),
            pl.BlockSpec((1, SSD_CONV_DIM), const),
            pl.BlockSpec((1, LANE), const),
            pl.BlockSpec((1, LANE), const),
            pl.BlockSpec((1, SSD_DINNER), const),
            pl.BlockSpec((1, SSD_DINNER), const),
        ],
        out_specs=pl.BlockSpec((L, SSD_DINNER), row(0)),
        out_shape=jax.ShapeDtypeStruct((B * S, SSD_DINNER), BF16),
        scratch_shapes=[pltpu.VMEM((L + 8, SSD_CONV_DIM), F32),
                        pltpu.VMEM((SSD_H, SSD_HD, SSD_N), F32),
                        pltpu.VMEM((L, SSD_DINNER), F32)],
        compiler_params=_cparams(("parallel", "arbitrary")),
        name="ssd",
    )(u, u, u, cw, cb, dtb, a_pad, d_exp, ng)


def _rope64(x, cm, s1, s2):
    return x * cm + pltpu.roll(x, LANE - MLA_ROPE // 2, 1) * s1 + pltpu.roll(x, MLA_ROPE // 2, 1) * s2


def _mla_prep_kernel(c_ref, sm_ref, cm_ref, s1_ref, s2_ref, wq_ref, wkv_ref, qng_ref, kvng_ref,
                     qhg_ref, khg_ref, q_ref, k_ref, v_ref):
    blk = c_ref[...]
    cq = blk[:, :MLA_Q_LORA]
    ckv = blk[:, MLA_Q_LORA:]
    cqn = cq * lax.rsqrt(jnp.mean(cq * cq, axis=-1, keepdims=True) + EPS) * qng_ref[...]
    ckvn = ckv * lax.rsqrt(jnp.mean(ckv * ckv, axis=-1, keepdims=True) + EPS) * kvng_ref[...]
    qf = _dot(cqn.astype(BF16), wq_ref[...])
    kvf = _dot(ckvn.astype(BF16), wkv_ref[...])
    cm, s1, s2 = cm_ref[...], s1_ref[...], s2_ref[...]
    lane = lax.broadcasted_iota(jnp.int32, (1, LANE), 1)
    kr = jnp.where(lane < MLA_ROPE, sm_ref[...], 0.0)
    kr_ss = jnp.sum(kr * kr, axis=-1, keepdims=True)
    qhg, khg = qhg_ref[...], khg_ref[...]
    for h in range(MLA_H):
        qn = qf[:, 256 * h:256 * h + LANE]
        qr = qf[:, 256 * h + LANE:256 * (h + 1)]
        rq = lax.rsqrt((jnp.sum(qn * qn, axis=-1, keepdims=True)
                        + jnp.sum(qr * qr, axis=-1, keepdims=True)) * (1.0 / MLA_QK) + EPS)
        rq = rq * (MLA_QK ** -0.5 * LOG2E)
        q_ref[0, h, :, 0:LANE] = (qn * rq * qhg[:, :LANE]).astype(BF16)
        q_ref[0, h, :, LANE:2 * LANE] = _rope64(qr * rq * qhg[:, LANE:], cm, s1, s2).astype(BF16)
        kn = kvf[:, 256 * h:256 * h + LANE]
        rk = lax.rsqrt((jnp.sum(kn * kn, axis=-1, keepdims=True) + kr_ss) * (1.0 / MLA_QK) + EPS)
        k_ref[0, h, :, 0:LANE] = (kn * rk * khg[:, :LANE]).astype(BF16)
        k_ref[0, h, :, LANE:2 * LANE] = _rope64(kr * rk * khg[:, LANE:], cm, s1, s2).astype(BF16)
        v_ref[0, h, :, :] = kvf[:, 256 * h + LANE:256 * (h + 1)].astype(BF16)


def _mla_prep(u, cm, s1, s2, wq, wkv, qng, kvng, qhg, khg, B, S, ts=512):
    ts = min(ts, S)
    nb = S // ts
    row = lambda c: (lambda b, i: (b * nb + i, c))
    const = lambda b, i: (0, 0)
    hspec = lambda w: pl.BlockSpec((1, MLA_H, ts, w), lambda b, i: (b, 0, i, 0))
    return pl.pallas_call(
        _mla_prep_kernel,
        grid=(B, nb),
        in_specs=[
            pl.BlockSpec((ts, 512), row(COL_CQKV // 512)),
            pl.BlockSpec((ts, LANE), row(COL_SMALL // LANE)),
            pl.BlockSpec((ts, LANE), row(0)), pl.BlockSpec((ts, LANE), row(0)), pl.BlockSpec((ts, LANE), row(0)),
            pl.BlockSpec((MLA_Q_LORA, MLA_H * 256), const),
            pl.BlockSpec((MLA_KV_LORA, MLA_H * 256), const),
            pl.BlockSpec((1, MLA_Q_LORA), const),
            pl.BlockSpec((1, MLA_KV_LORA), const),
            pl.BlockSpec((1, 256), const),
            pl.BlockSpec((1, 256), const),
        ],
        out_specs=[hspec(256), hspec(256), hspec(MLA_V)],
        out_shape=[jax.ShapeDtypeStruct((B, MLA_H, S, 256), BF16),
                   jax.ShapeDtypeStruct((B, MLA_H, S, 256), BF16),
                   jax.ShapeDtypeStruct((B, MLA_H, S, MLA_V), BF16)],
        compiler_params=_cparams(("parallel", "parallel")),
        name="mla_prep",
    )(u, u, cm, s1, s2, wq, wkv, qng, kvng, qhg, khg)


def _flash_kernel(q_ref, k_ref, v_ref, g_ref, o_ref, m_ref, l_ref, acc_ref, sa_ref, sb_ref, *, t):
    i = pl.program_id(2)
    m_ref[...] = jnp.full_like(m_ref, -jnp.inf)
    l_ref[...] = jnp.zeros_like(l_ref)
    acc_ref[...] = jnp.zeros_like(acc_ref)
    q = q_ref[0, 0]

    def rows(j):
        return pl.ds(pl.multiple_of(j * t, t), t)

    def scores(j):
        return _dot_nt(q, k_ref[0, 0, rows(j), :])

    def masked(s):
        causal = lax.broadcasted_iota(jnp.int32, (t, t), 1) <= lax.broadcasted_iota(jnp.int32, (t, t), 0)
        return jnp.where(causal, s, -jnp.inf)

    def update(s, j):
        m_old = m_ref[...]
        m_new = jnp.maximum(m_old, jnp.max(s, axis=-1, keepdims=True))
        alpha = jnp.exp2(m_old - m_new)
        p = jnp.exp2(s - jnp.concatenate([m_new] * (t // LANE), axis=1))
        l_ref[...] = alpha * l_ref[...] + jnp.sum(p, axis=-1, keepdims=True)
        acc_ref[...] = alpha * acc_ref[...] + _dot(p.astype(BF16), v_ref[0, 0, rows(j), :])
        m_ref[...] = m_new

    sa_ref[...] = scores(0)

    def pair(p, carry):
        j = 2 * p
        s = sa_ref[...]
        sb_ref[...] = scores(j + 1)
        update(s, j)
        s = sb_ref[...]
        sa_ref[...] = scores(j + 2)
        update(s, j + 1)
        return carry

    lax.fori_loop(0, i // 2, pair, 0)

    @pl.when(i % 2 == 1)
    def _():
        s = sa_ref[...]
        sb_ref[...] = scores(i)
        update(s, i - 1)
        update(masked(sb_ref[...]), i)

    @pl.when(i % 2 == 0)
    def _():
        update(masked(sa_ref[...]), i)

    o_ref[...] = (acc_ref[...] / l_ref[...] * _silu(g_ref[...])).astype(BF16)


def _flash(q, k, v, u, B, S, t=512):
    t = min(t, S)
    nq = S // t
    return pl.pallas_call(
        functools.partial(_flash_kernel, t=t),
        grid=(B, MLA_H, nq),
        in_specs=[
            pl.BlockSpec((1, 1, t, 256), lambda b, h, i: (b, h, i, 0)),
            pl.BlockSpec((1, 1, S, 256), lambda b, h, i: (b, h, 0, 0)),
            pl.BlockSpec((1, 1, S, MLA_V), lambda b, h, i: (b, h, 0, 0)),
            pl.BlockSpec((t, LANE), lambda b, h, i: (b * nq + i, COL_GC // LANE + h)),
        ],
        out_specs=pl.BlockSpec((t, LANE), lambda b, h, i: (b * nq + i, h)),
        out_shape=jax.ShapeDtypeStruct((B * S, MLA_H * MLA_V), BF16),
        scratch_shapes=[pltpu.VMEM((t, LANE), F32), pltpu.VMEM((t, LANE), F32), pltpu.VMEM((t, MLA_V), F32),
                        pltpu.VMEM((t, t), F32), pltpu.VMEM((t, t), F32)],
        compiler_params=_cparams(("parallel", "parallel", "arbitrary")),
        name="mla_flash",
    )(q, k, v, u)


def _dil_kernel(q_ref, k_ref, v_ref, g_ref, cd_ref, sd_ref, qg_ref, kg_ref, o_ref,
                qs_ref, ks_ref, qd_ref, kd_ref, vd_ref, oc_ref, ec_ref, tmp_ref, *, S):
    Q = DIL_BLK
    cd, sd = cd_ref[...], sd_ref[...]

    def norm_rope(x, g):
        xn = x * lax.rsqrt(jnp.mean(x * x, axis=-1, keepdims=True) + EPS) * g
        return xn * cd + pltpu.roll(xn, DIL_HD // 2, 1) * sd

    qs_ref[...] = norm_rope(q_ref[...], qg_ref[...] * (DIL_HD ** -0.5 * LOG2E))
    ks_ref[...] = norm_rope(k_ref[...], kg_ref[...])

    ra = lax.broadcasted_iota(jnp.int32, (Q, 2 * Q), 0)
    cc = lax.broadcasted_iota(jnp.int32, (Q, 2 * Q), 1)
    band = (cc >= ra) & (cc <= ra + Q)
    bias_inner = jnp.where(band, 0.0, -jnp.inf)
    bias_first = jnp.where(band & (cc >= Q), 0.0, -jnp.inf)
    n_units = S // Q

    for c, (w, d) in enumerate(DIL_CONFIGS):
        assert w // d == Q
        n_sub = S // d
        pitch = n_sub + Q
        nblk = n_sub // Q
        piece = min(n_sub, 512)
        streams = ((qs_ref, qd_ref, n_sub, 0), (ks_ref, kd_ref, pitch, Q), (v_ref, vd_ref, pitch, Q))
        for r in range(d):
            kd_ref[r * pitch:r * pitch + Q, :] = jnp.zeros((Q, DIL_HD), BF16)
            vd_ref[r * pitch:r * pitch + Q, :] = jnp.zeros((Q, DIL_HD), BF16)
        if d <= DIL_DIRECT_STRIDE:
            for r in range(d):
                for c0 in range(0, n_sub, piece):
                    src = pl.ds(r + c0 * d, piece, stride=d)
                    for src_ref, dst_ref, cpitch, lead in streams:
                        dst = r * cpitch + lead + c0
                        dst_ref[dst:dst + piece, :] = src_ref[src, :].astype(BF16)
        else:
            outer, inner = DIL_DIRECT_STRIDE, d // DIL_DIRECT_STRIDE
            n_outer = S // outer
            assert d % outer == 0 and inner <= DIL_DIRECT_STRIDE
            for r0 in range(outer):
                for src_ref, dst_ref, cpitch, lead in streams:
                    step = min(n_outer, 512)
                    for c0 in range(0, n_outer, step):
                        tmp_ref[c0:c0 + step, :] = src_ref[pl.ds(r0 + c0 * outer, step, stride=outer), :]
                    for m in range(inner):
                        dst = (r0 + outer * m) * cpitch + lead
                        dst_ref[dst:dst + n_sub, :] = tmp_ref[pl.ds(m, n_sub, stride=inner), :].astype(BF16)

        unroll = min(DIL_UNROLL, n_units)

        def group(n0, carry, c=c, n_sub=n_sub, pitch=pitch, nblk=nblk, unroll=unroll):
            for uu in range(unroll):
                n = n0 * unroll + uu
                r = n // nblk
                i = n - r * nblk
                qrows = pl.ds(pl.multiple_of(r * n_sub + i * Q, Q), Q)
                krows = pl.ds(pl.multiple_of(r * pitch + i * Q, Q), 2 * Q)
                s = _dot_nt(qd_ref[qrows, :], kd_ref[krows, :]) + jnp.where(i > 0, bias_inner, bias_first)
                m = jnp.max(s, axis=-1, keepdims=True)
                p = jnp.exp2(s - m)
                l = jnp.sum(p, axis=-1, keepdims=True)
                nat = pl.ds(i * (d * Q) + r, Q, stride=d) if d > 1 else pl.ds(pl.multiple_of(i * Q, Q), Q)
                oc_ref[c, nat, :] = _dot(p.astype(BF16), vd_ref[krows, :]) * (1.0 / l)
                ec_ref[c, nat, :] = jnp.broadcast_to(m + jnp.log2(l), (Q, LANE))
            return carry

        lax.fori_loop(0, n_units // unroll, group, 0)

    rows_per_step = min(S, 256)

    def merge(n, carry):
        rows = pl.ds(pl.multiple_of(n * rows_per_step, rows_per_step), rows_per_step)
        es = [ec_ref[c, rows, :] for c in range(len(DIL_CONFIGS))]
        e_max = functools.reduce(jnp.maximum, es)
        ws = [jnp.exp2(e - e_max) for e in es]
        num = sum(wt * oc_ref[c, rows, :] for c, wt in enumerate(ws))
        o_ref[rows, :] = (num / sum(ws) * _silu(g_ref[rows, :])).astype(BF16)
        return carry

    lax.fori_loop(0, S // rows_per_step, merge, 0)


def _dil(u, cd, sd, qg, kg, B, S):
    col = lambda c: (lambda b, h: (b, c // LANE + h))
    const = lambda b, h: (0, 0)
    blk = lambda im: pl.BlockSpec((S, LANE), im)
    kv_rows = max(S + d * DIL_BLK for _, d in DIL_CONFIGS)
    table = pl.BlockSpec((S, LANE), lambda b, h: (b, 0), pipeline_mode=pl.Buffered(1))
    return pl.pallas_call(
        functools.partial(_dil_kernel, S=S),
        grid=(B, DIL_H),
        in_specs=[blk(col(COL_DQ)), blk(col(COL_DK)), blk(col(COL_DV)), blk(col(COL_GD)),
                  table, table,
                  pl.BlockSpec((1, DIL_HD), const), pl.BlockSpec((1, DIL_HD), const)],
        out_specs=blk(lambda b, h: (b, h)),
        out_shape=jax.ShapeDtypeStruct((B * S, DIL_H * DIL_HD), BF16),
        scratch_shapes=[pltpu.VMEM((S, DIL_HD), F32), pltpu.VMEM((S, DIL_HD), F32),
                        pltpu.VMEM((S, DIL_HD), BF16),
                        pltpu.VMEM((kv_rows, DIL_HD), BF16), pltpu.VMEM((kv_rows, DIL_HD), BF16),
                        pltpu.VMEM((len(DIL_CONFIGS), S, DIL_HD), F32),
                        pltpu.VMEM((len(DIL_CONFIGS), S, LANE), F32),
                        pltpu.VMEM((S // DIL_DIRECT_STRIDE, DIL_HD), F32)],
        compiler_params=_cparams(("parallel", "parallel")),
        name="dilated",
    )(u, u, u, u, cd, sd, qg, kg)


SRC_ALR, SRC_XBC, SRC_DT, SRC_CQKV, SRC_KR, SRC_DQKV = 3072, 3088, 4112, 4120, 4632, 4696
N_IN = 6232
PACK_RUNS = ((0, COL_XBC, 0), (SRC_XBC, SSD_CONV_DIM, COL_XBC), (SRC_CQKV, 512, COL_CQKV),
             (SRC_DQKV, 3 * 512, COL_DQ))


def _pack_kernel(w_ref, o_ref):
    w = w_ref[0]
    for src, width, dst in PACK_RUNS:
        o_ref[0, :, dst:dst + width] = w[:, src:src + width].astype(BF16)
    small = jnp.concatenate(
        [w[:, SRC_KR:SRC_KR + MLA_ROPE], w[:, SRC_ALR:SRC_ALR + GLA_GATE_RANK], w[:, SRC_DT:SRC_DT + SSD_H],
         jnp.zeros((w.shape[0], LANE - SM_DT - SSD_H), w.dtype)], axis=1)
    o_ref[0, :, COL_SMALL:COL_SMALL + LANE] = small.astype(BF16)


def _pack_w_in(w, tr=256):
    depth, rows, n_in = w.shape
    assert n_in == N_IN and rows % tr == 0
    return pl.pallas_call(
        _pack_kernel,
        grid=(depth, rows // tr),
        in_specs=[pl.BlockSpec((1, tr, N_IN), lambda l, i: (l, i, 0))],
        out_specs=pl.BlockSpec((1, tr, N_PACK), lambda l, i: (l, i, 0)),
        out_shape=jax.ShapeDtypeStruct((depth, rows, N_PACK), BF16),
        compiler_params=_cparams(("parallel", "parallel")),
        name="pack_w_in",
    )(w)


def _row(v, width=None, offset=0):
    v = v.astype(F32).reshape(1, -1)
    if width is None:
        return v
    return jnp.pad(v, ((0, 0), (offset, width - offset - v.shape[1])))


def _pad_heads(w, n_heads, real, padded):
    k = w.shape[0]
    w = w.reshape(k, n_heads, real)
    return jnp.pad(w, ((0, 0), (0, 0), (0, padded - real))).reshape(k, n_heads * padded)


def kernel(x, positions, ln_g, w_in, w_out, gla_gate_w2, gla_gate_b, gla_norm_g, ssd_conv_w, ssd_conv_b,
           ssd_dt_bias, ssd_A_log, ssd_D, ssd_norm_g, mla_q_norm_g, mla_kv_norm_g, mla_w_uq, mla_w_ukv,
           mla_q_head_g, mla_k_head_g, dil_q_g, dil_k_g):
    B, S, D = x.shape
    depth = w_in.shape[0]
    assert D == D_MODEL and S % SSD_CHUNK == 0 and S % GLA_ROWS == 0
    assert all(S % (d * DIL_BLK) == 0 for _, d in DIL_CONFIGS)
    T = B * S
    cm, s1, s2, cd, sd = _rope_tables(positions)
    w_in_packed = _pack_w_in(w_in)
    xf = x.reshape(T, D)
    for l in range(depth):
        u = _inproj(xf, _row(ln_g[l]), w_in_packed, l)
        w2p = jnp.pad(gla_gate_w2[l], ((SM_LR, LANE - SM_LR - GLA_GATE_RANK), (0, 0))).astype(BF16)
        ya = _gla(u, w2p, _row(gla_gate_b[l]), _row(gla_norm_g[l]), B, S)
        yb = _ssd(u, ssd_conv_w[l].astype(F32), _row(ssd_conv_b[l]),
                  _row(ssd_dt_bias[l], LANE, SM_DT), _row(ssd_A_log[l], LANE, SM_DT),
                  _row(jnp.repeat(ssd_D[l], SSD_HD)), _row(ssd_norm_g[l]), B, S)
        q, k, v = _mla_prep(u, cm, s1, s2,
                            _pad_heads(mla_w_uq[l], MLA_H, MLA_QK, 256).astype(BF16),
                            mla_w_ukv[l].astype(BF16), _row(mla_q_norm_g[l]), _row(mla_kv_norm_g[l]),
                            _row(mla_q_head_g[l], 256), _row(mla_k_head_g[l], 256), B, S)
        yc = _flash(q, k, v, u, B, S)
        yd = _dil(u, cd, sd, _row(dil_q_g[l]), _row(dil_k_g[l]), B, S)
        xf = _outproj(xf, ya, yb, yc, yd, w_out[l].astype(BF16))
    return xf.reshape(B, S, D)
```

```python
import functools
import math

import jax
import jax.numpy as jnp
import numpy as np
from jax import lax
from jax.experimental import pallas as pl
from jax.experimental.pallas import tpu as pltpu

F32 = jnp.float32
BF16 = jnp.bfloat16

D_MODEL = 2048
GLA_H, GLA_DK, GLA_DV = 4, 64, 128
GLA_GATE_RANK = 16
GLA_GATE_TAU = 16.0
SSD_DINNER, SSD_HD, SSD_G, SSD_N, SSD_CONV = 512, 64, 2, 128, 4
SSD_H = SSD_DINNER // SSD_HD
SSD_CONV_DIM = SSD_DINNER + 2 * SSD_G * SSD_N
MLA_H, MLA_NOPE, MLA_ROPE, MLA_V = 4, 128, 64, 128
MLA_QK = MLA_NOPE + MLA_ROPE
MLA_Q_LORA, MLA_KV_LORA = 384, 128
DIL_H, DIL_HD = 4, 128
DIL_CONFIGS = ((128, 1), (512, 4), (2048, 16))
ROPE_THETA = 10000.0
EPS = 1e-6
LOG2E = math.log2(math.e)
D_MIX = GLA_H * GLA_DV + SSD_DINNER + MLA_H * MLA_V + DIL_H * DIL_HD

LANE = 128
VMEM_LIMIT = 52 * 1024 * 1024

COL_GA, COL_GB, COL_GC, COL_GD = 0, 512, 1024, 1536
COL_AQK = 2048
COL_AV = 2560
COL_XBC = 3072
COL_CQKV = 4096
COL_DQ, COL_DK, COL_DV = 4608, 5120, 5632
COL_SMALL = 6144
N_PACK = 6272
SM_KR, SM_LR, SM_DT = 0, 64, 80

GLA_CHUNK = 32
GLA_ROWS = 256
SSD_CHUNK = 256
DIL_BLK = 128
DIL_UNROLL = 16
DIL_DIRECT_STRIDE = 4


def _cparams(sem):
    return pltpu.CompilerParams(dimension_semantics=sem, vmem_limit_bytes=VMEM_LIMIT)


def _silu(x):
    return x * (1.0 / (1.0 + jnp.exp(-x)))


def _dot(a, b):
    return jnp.dot(a, b, preferred_element_type=F32)


def _dot_nt(a, b):
    return lax.dot_general(a, b, (((1,), (1,)), ((), ())), preferred_element_type=F32)


def _dot_tn(a, b):
    return lax.dot_general(a, b, (((0,), (0,)), ((), ())), preferred_element_type=F32)


def _split_dot(tri, x):
    hi = x.astype(BF16)
    lo = (x - hi.astype(F32)).astype(BF16)
    return _dot(tri, hi) + _dot(tri, lo)


def _rope_tables_kernel(pos_ref, f_ref, cm_ref, s1_ref, s2_ref, cd_ref, sd_ref):
    half = LANE // 2
    ang = pos_ref[...] * f_ref[...]
    cos, sin = jnp.cos(ang), jnp.sin(ang)
    lane = lax.broadcasted_iota(jnp.int32, (1, LANE), 1)
    low = lane < half
    cm_ref[...] = jnp.where(low, cos, 0.0)
    s1_ref[...] = jnp.where(lane < MLA_ROPE // 2, -sin, 0.0)
    s2_ref[...] = jnp.where((lane >= MLA_ROPE // 2) & low, sin, 0.0)
    cd_ref[...] = jnp.where(low, pltpu.roll(cos, half, 1), cos)
    sd_ref[...] = jnp.where(low, -pltpu.roll(sin, half, 1), sin)


def _rope_tables(positions):
    assert MLA_ROPE == LANE // 2 and DIL_HD == LANE
    T = positions.size
    ts = min(T, 2048)
    pos = positions.reshape(T, 1).astype(F32)
    lane = np.arange(LANE // 2)
    fm = np.exp(-math.log(ROPE_THETA) * (lane % (MLA_ROPE // 2)) * (2.0 / MLA_ROPE))
    fd = np.exp(-math.log(ROPE_THETA) * lane * (2.0 / DIL_HD))
    freqs = jnp.asarray(np.concatenate([fm, fd]), F32).reshape(1, LANE)
    row = pl.BlockSpec((ts, LANE), lambda i: (i, 0))
    return pl.pallas_call(
        _rope_tables_kernel,
        grid=(T // ts,),
        in_specs=[pl.BlockSpec((ts, 1), lambda i: (i, 0)), pl.BlockSpec((1, LANE), lambda i: (0, 0))],
        out_specs=[row] * 5,
        out_shape=[jax.ShapeDtypeStruct((T, LANE), F32)] * 5,
        compiler_params=_cparams(("parallel",)),
        name="rope_tables",
    )(pos, freqs)


def _inproj_kernel(x_ref, g_ref, w_ref, o_ref, h_ref):
    @pl.when(pl.program_id(1) == 0)
    def _():
        x = x_ref[...]
        ms = jnp.mean(x * x, axis=-1, keepdims=True)
        h_ref[...] = (x * lax.rsqrt(ms + EPS) * g_ref[...]).astype(BF16)

    o_ref[...] = _dot(h_ref[...], w_ref[...])


def _inproj(x, g, w, layer, tm=1024, tn=896):
    T = x.shape[0]
    tm = min(tm, T)
    return pl.pallas_call(
        _inproj_kernel,
        grid=(T // tm, N_PACK // tn),
        in_specs=[
            pl.BlockSpec((tm, D_MODEL), lambda i, j: (i, 0)),
            pl.BlockSpec((1, D_MODEL), lambda i, j: (0, 0)),
            pl.BlockSpec((None, D_MODEL, tn), lambda i, j: (layer, 0, j)),
        ],
        out_specs=pl.BlockSpec((tm, tn), lambda i, j: (i, j)),
        out_shape=jax.ShapeDtypeStruct((T, N_PACK), F32),
        scratch_shapes=[pltpu.VMEM((tm, D_MODEL), BF16)],
        compiler_params=_cparams(("parallel", "arbitrary")),
        name="inproj",
    )(x, g, w)


def _outproj_kernel(x_ref, ya_ref, yb_ref, yc_ref, yd_ref, w_ref, o_ref):
    acc = x_ref[...]
    for n, y_ref in enumerate((ya_ref, yb_ref, yc_ref, yd_ref)):
        acc = acc + _dot(y_ref[...], w_ref[n * 512:(n + 1) * 512, :])
    o_ref[...] = acc


def _outproj(x, ya, yb, yc, yd, w, layer, tm=256):
    T = x.shape[0]
    tm = min(tm, T)
    yspec = pl.BlockSpec((tm, 512), lambda i: (i, 0))
    return pl.pallas_call(
        _outproj_kernel,
        grid=(T // tm,),
        in_specs=[pl.BlockSpec((tm, D_MODEL), lambda i: (i, 0)), yspec, yspec, yspec, yspec,
                  pl.BlockSpec((None, D_MIX, D_MODEL), lambda i: (layer, 0, 0))],
        out_specs=pl.BlockSpec((tm, D_MODEL), lambda i: (i, 0)),
        out_shape=jax.ShapeDtypeStruct((T, D_MODEL), F32),
        compiler_params=_cparams(("parallel",)),
        name="outproj",
    )(x, ya, yb, yc, yd, w)


def _gla_kernel(qk_ref, v_ref, gate_ref, sm_ref, w2_ref, b2_ref, ng_ref, o_ref, st_ref):
    @pl.when(pl.program_id(1) == 0)
    def _():
        st_ref[...] = jnp.zeros_like(st_ref)

    R, C = GLA_ROWS, GLA_CHUNK
    xg = _dot(sm_ref[...].astype(BF16), w2_ref[...]) + b2_ref[...]
    logd = (jnp.minimum(xg, 0.0) - jnp.log(1.0 + jnp.exp(-jnp.abs(xg)))) * (1.0 / GLA_GATE_TAU)
    ri = lax.broadcasted_iota(jnp.int32, (R, R), 0)
    ci = lax.broadcasted_iota(jnp.int32, (R, R), 1)
    back = ri - ci
    in_chunk = (back >= 0) & (back <= (ri & (C - 1)))
    bc_all = _split_dot(jnp.where(in_chunk, 1.0, 0.0).astype(BF16), logd)
    ng = ng_ref[...]
    n_chunks = R // C
    hk = GLA_H * GLA_DK
    bc = bc_all.reshape(n_chunks, C, hk)
    b_mid = bc[:, C // 2 - 1:C // 2, :]
    b_last = bc[:, C - 1:C, :]
    q = (qk_ref[:, :hk] * (GLA_DK ** -0.5)).reshape(n_chunks, C, hk)
    k = qk_ref[:, hk:].reshape(n_chunks, C, hk)
    qd = (q * jnp.exp(bc - b_mid)).astype(BF16).reshape(R, hk)
    kd = (k * jnp.exp(b_mid - bc)).astype(BF16).reshape(R, hk)
    qe = (q * jnp.exp(bc)).astype(BF16).reshape(R, hk)
    kl = (k * jnp.exp(b_last - bc)).astype(BF16).reshape(R, hk)
    e_last = jnp.exp(b_last)
    wide = n_chunks * GLA_DK
    place = ((lax.broadcasted_iota(jnp.int32, (R, wide), 0) >> (C.bit_length() - 1))
             == (lax.broadcasted_iota(jnp.int32, (R, wide), 1) >> (GLA_DK.bit_length() - 1)))

    def block_diag(x):
        pair = jnp.concatenate([x, x], axis=1)
        return jnp.where(place, jnp.concatenate([pair] * (n_chunks // 2), axis=1), jnp.zeros((), x.dtype))

    for h in range(GLA_H):
        kcols = slice(h * GLA_DK, (h + 1) * GLA_DK)
        vcols = slice(h * GLA_DV, (h + 1) * GLA_DV)
        v = v_ref[:, vcols].astype(BF16)
        attn = jnp.where(in_chunk, _dot_nt(qd[:, kcols], kd[:, kcols]), 0.0).astype(BF16)
        u_all = _dot_tn(v, block_diag(kl[:, kcols]))
        st = st_ref[h]
        entering = []
        for c in range(n_chunks):
            entering.append(st)
            st = st * e_last[c][:, kcols] + u_all[:, c * GLA_DK:(c + 1) * GLA_DK]
        st_ref[h] = st
        s_all = jnp.concatenate(entering, axis=1).astype(BF16)
        o = _dot(attn, v) + _dot_nt(block_diag(qe[:, kcols]), s_all)
        ms = jnp.mean(o * o, axis=-1, keepdims=True)
        y = o * lax.rsqrt(ms + EPS) * ng * _silu(gate_ref[:, vcols])
        o_ref[:, vcols] = y.astype(BF16)


def _gla(u, w2p, b2, ng, B, S):
    R = GLA_ROWS
    nb = S // R
    row = lambda c: (lambda b, i: (b * nb + i, c))
    const = lambda b, i: (0, 0)
    return pl.pallas_call(
        _gla_kernel,
        grid=(B, nb),
        in_specs=[
            pl.BlockSpec((R, 512), row(COL_AQK // 512)),
            pl.BlockSpec((R, 512), row(COL_AV // 512)),
            pl.BlockSpec((R, 512), row(COL_GA // 512)),
            pl.BlockSpec((R, LANE), row(COL_SMALL // LANE)),
            pl.BlockSpec((LANE, GLA_H * GLA_DK), const),
            pl.BlockSpec((1, GLA_H * GLA_DK), const),
            pl.BlockSpec((1, GLA_DV), const),
        ],
        out_specs=pl.BlockSpec((R, 512), row(0)),
        out_shape=jax.ShapeDtypeStruct((B * S, GLA_H * GLA_DV), BF16),
        scratch_shapes=[pltpu.VMEM((GLA_H, GLA_DV, GLA_DK), F32)],
        compiler_params=_cparams(("parallel", "arbitrary")),
        name="gla",
    )(u, u, u, u, w2p, b2, ng)


def _ssd_kernel(xbc_ref, z_ref, sm_ref, cw_ref, cb_ref, dtb_ref, a_ref, d_ref, ng_ref, o_ref,
                xpad_ref, st_ref, y_ref):
    L = SSD_CHUNK
    HALO = 8

    @pl.when(pl.program_id(1) == 0)
    def _():
        st_ref[...] = jnp.zeros_like(st_ref)
        xpad_ref[0:HALO, :] = jnp.zeros((HALO, SSD_CONV_DIM), F32)

    xpad_ref[HALO:HALO + L, :] = xbc_ref[...]
    conv = cb_ref[...] + jnp.zeros((L, SSD_CONV_DIM), F32)
    for j in range(SSD_CONV):
        conv = conv + xpad_ref[pl.ds(HALO - (SSD_CONV - 1) + j, L), :] * cw_ref[j:j + 1, :]
    xpad_ref[0:HALO, :] = xpad_ref[L:L + HALO, :]
    xact = _silu(conv)

    lane = lax.broadcasted_iota(jnp.int32, (1, LANE), 1)
    is_dt = (lane >= SM_DT) & (lane < SM_DT + SSD_H)
    pre = sm_ref[...] + dtb_ref[...]
    dt_all = jnp.maximum(pre, 0.0) + jnp.log(1.0 + jnp.exp(-jnp.abs(pre)))
    a_all = jnp.where(is_dt, dt_all * -jnp.exp(a_ref[...]), 0.0)
    ri = lax.broadcasted_iota(jnp.int32, (L, L), 0)
    ci = lax.broadcasted_iota(jnp.int32, (L, L), 1)
    lower = ci <= ri
    cs_all = _split_dot(jnp.where(lower, 1.0, 0.0).astype(BF16), a_all)
    cs_t = cs_all.T

    heads_per_group = SSD_H // SSD_G
    for g in range(SSD_G):
        bm = xact[:, SSD_DINNER + g * SSD_N:SSD_DINNER + (g + 1) * SSD_N].astype(BF16)
        cm = xact[:, SSD_DINNER + SSD_G * SSD_N + g * SSD_N:
                  SSD_DINNER + SSD_G * SSD_N + (g + 1) * SSD_N].astype(BF16)
        scores = _dot_nt(cm, bm)
        for hh in range(heads_per_group):
            h = g * heads_per_group + hh
            xs = xact[:, h * SSD_HD:(h + 1) * SSD_HD]
            dt = dt_all[:, SM_DT + h:SM_DT + h + 1]
            cs_col = cs_all[:, SM_DT + h:SM_DT + h + 1]
            cs_row = cs_t[SM_DT + h:SM_DT + h + 1, :]
            cs_last = cs_col[L - 1:L, :]
            decay = jnp.where(lower, jnp.exp(cs_col - cs_row), 0.0)
            xdt = xs * dt
            y = _dot((scores * decay).astype(BF16), xdt.astype(BF16))
            prev = st_ref[h]
            y = y + _dot_nt(cm, prev.astype(BF16)) * jnp.exp(cs_col)
            st_ref[h] = prev * jnp.exp(cs_last) + _dot_tn(
                (xdt * jnp.exp(cs_last - cs_col)).astype(BF16), bm)
            y_ref[:, h * SSD_HD:(h + 1) * SSD_HD] = y + xs * d_ref[:, h * SSD_HD:(h + 1) * SSD_HD]

    yz = y_ref[...] * _silu(z_ref[...])
    gw = SSD_DINNER // SSD_G
    for g in range(SSD_G):
        yg = yz[:, g * gw:(g + 1) * gw]
        ms = jnp.mean(yg * yg, axis=-1, keepdims=True)
        o_ref[:, g * gw:(g + 1) * gw] = (yg * lax.rsqrt(ms + EPS) * ng_ref[:, g * gw:(g + 1) * gw]).astype(BF16)


def _ssd(u, cw, cb, dtb, a_pad, d_exp, ng, B, S):
    L = SSD_CHUNK
    nb = S // L
    row = lambda c: (lambda b, i: (b * nb + i, c))
    const = lambda b, i: (0, 0)
    return pl.pallas_call(
        _ssd_kernel,
        grid=(B, nb),
        in_specs=[
            pl.BlockSpec((L, SSD_CONV_DIM), row(COL_XBC // SSD_CONV_DIM)),
            pl.BlockSpec((L, 512), row(COL_GB // 512)),
            pl.BlockSpec((L, LANE), row(COL_SMALL // LANE)),
            pl.BlockSpec((SSD_CONV, SSD_CONV_DIM), const),
            pl.BlockSpec((1, SSD_CONV_DIM), const),
            pl.BlockSpec((1, LANE), const),
            pl.BlockSpec((1, LANE), const),
            pl.BlockSpec((1, SSD_DINNER), const),
            pl.BlockSpec((1, SSD_DINNER), const),
        ],
        out_specs=pl.BlockSpec((L, SSD_DINNER), row(0)),
        out_shape=jax.ShapeDtypeStruct((B * S, SSD_DINNER), BF16),
        scratch_shapes=[pltpu.VMEM((L + 8, SSD_CONV_DIM), F32),
                        pltpu.VMEM((SSD_H, SSD_HD, SSD_N), F32),
                        pltpu.VMEM((L, SSD_DINNER), F32)],
        compiler_params=_cparams(("parallel", "arbitrary")),
        name="ssd",
    )(u, u, u, cw, cb, dtb, a_pad, d_exp, ng)


def _rope64(x, cm, s1, s2):
    return x * cm + pltpu.roll(x, LANE - MLA_ROPE // 2, 1) * s1 + pltpu.roll(x, MLA_ROPE // 2, 1) * s2


def _mla_prep_kernel(c_ref, sm_ref, cm_ref, s1_ref, s2_ref, wq_ref, wkv_ref, qng_ref, kvng_ref,
                     qhg_ref, khg_ref, q_ref, k_ref, v_ref):
    blk = c_ref[...]
    cq = blk[:, :MLA_Q_LORA]
    ckv = blk[:, MLA_Q_LORA:]
    def lane_tile_sum(x):
        tiles = [x[:, c:c + LANE] for c in range(0, x.shape[1], LANE)]
        return jnp.sum(functools.reduce(lambda a, b: a + b, tiles), axis=-1, keepdims=True)

    cqn = cq * lax.rsqrt(lane_tile_sum(cq * cq) * (1.0 / MLA_Q_LORA) + EPS) * qng_ref[...]
    ckvn = ckv * lax.rsqrt(lane_tile_sum(ckv * ckv) * (1.0 / MLA_KV_LORA) + EPS) * kvng_ref[...]
    qf = _dot(cqn.astype(BF16), wq_ref[...])
    kvf = _dot(ckvn.astype(BF16), wkv_ref[...])
    cm, s1, s2 = cm_ref[...], s1_ref[...], s2_ref[...]
    lane = lax.broadcasted_iota(jnp.int32, (1, LANE), 1)
    kr = jnp.where(lane < MLA_ROPE, sm_ref[...], 0.0)
    kr_sq = kr * kr
    qhg, khg = qhg_ref[...], khg_ref[...]
    kr_rot = _rope64(kr * khg[:, LANE:], cm, s1, s2)
    for h in range(MLA_H):
        qh = qf[:, 256 * h:256 * (h + 1)]
        rq = lax.rsqrt(lane_tile_sum(qh * qh) * (1.0 / MLA_QK) + EPS)
        rq = rq * (MLA_QK ** -0.5 * LOG2E)
        q_ref[0, h, :, 0:LANE] = (qh[:, :LANE] * rq * qhg[:, :LANE]).astype(BF16)
        q_ref[0, h, :, LANE:2 * LANE] = _rope64(qh[:, LANE:] * rq * qhg[:, LANE:], cm, s1, s2).astype(BF16)
        kn = kvf[:, 256 * h:256 * h + LANE]
        rk = lax.rsqrt(jnp.sum(kn * kn + kr_sq, axis=-1, keepdims=True) * (1.0 / MLA_QK) + EPS)
        k_ref[0, h, :, 0:LANE] = (kn * rk * khg[:, :LANE]).astype(BF16)
        k_ref[0, h, :, LANE:2 * LANE] = (kr_rot * rk).astype(BF16)
        v_ref[0, h, :, :] = kvf[:, 256 * h + LANE:256 * (h + 1)].astype(BF16)


def _mla_prep(u, cm, s1, s2, wq, wkv, qng, kvng, qhg, khg, B, S, ts=512):
    ts = min(ts, S)
    nb = S // ts
    row = lambda c: (lambda b, i: (b * nb + i, c))
    const = lambda b, i: (0, 0)
    hspec = lambda w: pl.BlockSpec((1, MLA_H, ts, w), lambda b, i: (b, 0, i, 0))
    return pl.pallas_call(
        _mla_prep_kernel,
        grid=(B, nb),
        in_specs=[
            pl.BlockSpec((ts, 512), row(COL_CQKV // 512)),
            pl.BlockSpec((ts, LANE), row(COL_SMALL // LANE)),
            pl.BlockSpec((ts, LANE), row(0)), pl.BlockSpec((ts, LANE), row(0)), pl.BlockSpec((ts, LANE), row(0)),
            pl.BlockSpec((MLA_Q_LORA, MLA_H * 256), const),
            pl.BlockSpec((MLA_KV_LORA, MLA_H * 256), const),
            pl.BlockSpec((1, MLA_Q_LORA), const),
            pl.BlockSpec((1, MLA_KV_LORA), const),
            pl.BlockSpec((1, 256), const),
            pl.BlockSpec((1, 256), const),
        ],
        out_specs=[hspec(256), hspec(256), hspec(MLA_V)],
        out_shape=[jax.ShapeDtypeStruct((B, MLA_H, S, 256), BF16),
                   jax.ShapeDtypeStruct((B, MLA_H, S, 256), BF16),
                   jax.ShapeDtypeStruct((B, MLA_H, S, MLA_V), BF16)],
        compiler_params=_cparams(("parallel", "parallel")),
        name="mla_prep",
    )(u, u, cm, s1, s2, wq, wkv, qng, kvng, qhg, khg)


def _flash_kernel(q_ref, k_ref, v_ref, g_ref, o_ref, m_ref, l_ref, acc_ref, sa_ref, sb_ref, *, t):
    i = pl.program_id(2)
    m_ref[...] = jnp.full_like(m_ref, -jnp.inf)
    l_ref[...] = jnp.zeros_like(l_ref)
    acc_ref[...] = jnp.zeros_like(acc_ref)
    q = q_ref[0, 0]

    def rows(j):
        return pl.ds(pl.multiple_of(j * t, t), t)

    def scores(j):
        return _dot_nt(q, k_ref[0, 0, rows(j), :])

    def masked(s):
        causal = lax.broadcasted_iota(jnp.int32, (t, t), 1) <= lax.broadcasted_iota(jnp.int32, (t, t), 0)
        return jnp.where(causal, s, -jnp.inf)

    def update(s, j):
        m_old = m_ref[...]
        m_new = jnp.maximum(m_old, jnp.max(s, axis=-1, keepdims=True))
        alpha = jnp.exp2(m_old - m_new)
        p = jnp.exp2(s - jnp.concatenate([m_new] * (t // LANE), axis=1))
        l_ref[...] = alpha * l_ref[...] + jnp.sum(p, axis=-1, keepdims=True)
        acc_ref[...] = alpha * acc_ref[...] + _dot(p.astype(BF16), v_ref[0, 0, rows(j), :])
        m_ref[...] = m_new

    sa_ref[...] = scores(0)

    def pair(p, carry):
        j = 2 * p
        s = sa_ref[...]
        sb_ref[...] = scores(j + 1)
        update(s, j)
        s = sb_ref[...]
        sa_ref[...] = scores(j + 2)
        update(s, j + 1)
        return carry

    lax.fori_loop(0, i // 2, pair, 0)

    @pl.when(i % 2 == 1)
    def _():
        s = sa_ref[...]
        sb_ref[...] = scores(i)
        update(s, i - 1)
        update(masked(sb_ref[...]), i)

    @pl.when(i % 2 == 0)
    def _():
        update(masked(sa_ref[...]), i)

    o_ref[...] = (acc_ref[...] / l_ref[...] * _silu(g_ref[...])).astype(BF16)


def _flash(q, k, v, u, B, S, t=1024):
    t = min(t, S)
    nq = S // t
    return pl.pallas_call(
        functools.partial(_flash_kernel, t=t),
        grid=(B, MLA_H, nq),
        in_specs=[
            pl.BlockSpec((1, 1, t, 256), lambda b, h, i: (b, h, i, 0)),
            pl.BlockSpec((1, 1, S, 256), lambda b, h, i: (b, h, 0, 0)),
            pl.BlockSpec((1, 1, S, MLA_V), lambda b, h, i: (b, h, 0, 0)),
            pl.BlockSpec((t, LANE), lambda b, h, i: (b * nq + i, COL_GC // LANE + h)),
        ],
        out_specs=pl.BlockSpec((t, LANE), lambda b, h, i: (b * nq + i, h)),
        out_shape=jax.ShapeDtypeStruct((B * S, MLA_H * MLA_V), BF16),
        scratch_shapes=[pltpu.VMEM((t, LANE), F32), pltpu.VMEM((t, LANE), F32), pltpu.VMEM((t, MLA_V), F32),
                        pltpu.VMEM((t, t), F32), pltpu.VMEM((t, t), F32)],
        compiler_params=_cparams(("parallel", "parallel", "arbitrary")),
        name="mla_flash",
    )(q, k, v, u)


def _dil_kernel(q_ref, k_ref, v_ref, g_ref, cd_ref, sd_ref, qg_ref, kg_ref, o_ref,
                qs_ref, ks_ref, qd_ref, kd_ref, vd_ref, oc_ref, ec_ref, tmp_ref, *, S):
    Q = DIL_BLK
    cd, sd = cd_ref[...], sd_ref[...]

    def norm_rope(x, g):
        xn = x * lax.rsqrt(jnp.mean(x * x, axis=-1, keepdims=True) + EPS) * g
        return xn * cd + pltpu.roll(xn, DIL_HD // 2, 1) * sd

    qs_ref[...] = norm_rope(q_ref[...], qg_ref[...] * (DIL_HD ** -0.5 * LOG2E))
    ks_ref[...] = norm_rope(k_ref[...], kg_ref[...])

    ra = lax.broadcasted_iota(jnp.int32, (Q, 2 * Q), 0)
    cc = lax.broadcasted_iota(jnp.int32, (Q, 2 * Q), 1)
    band = (cc >= ra) & (cc <= ra + Q)
    bias_inner = jnp.where(band, 0.0, -jnp.inf)
    bias_first = jnp.where(band & (cc >= Q), 0.0, -jnp.inf)
    n_units = S // Q

    for c, (w, d) in enumerate(DIL_CONFIGS):
        assert w // d == Q
        n_sub = S // d
        pitch = n_sub + Q
        nblk = n_sub // Q
        piece = min(n_sub, 512)
        streams = ((qs_ref, qd_ref, n_sub, 0), (ks_ref, kd_ref, pitch, Q), (v_ref, vd_ref, pitch, Q))
        for r in range(d):
            kd_ref[r * pitch:r * pitch + Q, :] = jnp.zeros((Q, DIL_HD), BF16)
            vd_ref[r * pitch:r * pitch + Q, :] = jnp.zeros((Q, DIL_HD), BF16)
        if d <= DIL_DIRECT_STRIDE:
            for r in range(d):
                for c0 in range(0, n_sub, piece):
                    src = pl.ds(r + c0 * d, piece, stride=d)
                    for src_ref, dst_ref, cpitch, lead in streams:
                        dst = r * cpitch + lead + c0
                        dst_ref[dst:dst + piece, :] = src_ref[src, :].astype(BF16)
        else:
            outer, inner = DIL_DIRECT_STRIDE, d // DIL_DIRECT_STRIDE
            n_outer = S // outer
            assert d % outer == 0 and inner <= DIL_DIRECT_STRIDE
            for r0 in range(outer):
                for src_ref, dst_ref, cpitch, lead in streams:
                    step = min(n_outer, 512)
                    for c0 in range(0, n_outer, step):
                        tmp_ref[c0:c0 + step, :] = src_ref[pl.ds(r0 + c0 * outer, step, stride=outer), :]
                    for m in range(inner):
                        dst = (r0 + outer * m) * cpitch + lead
                        dst_ref[dst:dst + n_sub, :] = tmp_ref[pl.ds(m, n_sub, stride=inner), :].astype(BF16)

        unroll = min(DIL_UNROLL, n_units)

        def group(n0, carry, c=c, n_sub=n_sub, pitch=pitch, nblk=nblk, unroll=unroll):
            for uu in range(unroll):
                n = n0 * unroll + uu
                r = n // nblk
                i = n - r * nblk
                qrows = pl.ds(pl.multiple_of(r * n_sub + i * Q, Q), Q)
                krows = pl.ds(pl.multiple_of(r * pitch + i * Q, Q), 2 * Q)
                s = _dot_nt(qd_ref[qrows, :], kd_ref[krows, :]) + jnp.where(i > 0, bias_inner, bias_first)
                m = jnp.max(s, axis=-1, keepdims=True)
                p = jnp.exp2(s - m)
                l = jnp.sum(p, axis=-1, keepdims=True)
                nat = pl.ds(i * (d * Q) + r, Q, stride=d) if d > 1 else pl.ds(pl.multiple_of(i * Q, Q), Q)
                oc_ref[c, nat, :] = _dot(p.astype(BF16), vd_ref[krows, :]) * (1.0 / l)
                ec_ref[c, nat, :] = jnp.broadcast_to(m + jnp.log2(l), (Q, LANE))
            return carry

        lax.fori_loop(0, n_units // unroll, group, 0)

    rows_per_step = min(S, 256)

    def merge(n, carry):
        rows = pl.ds(pl.multiple_of(n * rows_per_step, rows_per_step), rows_per_step)
        es = [ec_ref[c, rows, :] for c in range(len(DIL_CONFIGS))]
        e_max = functools.reduce(jnp.maximum, es)
        ws = [jnp.exp2(e - e_max) for e in es]
        num = sum(wt * oc_ref[c, rows, :] for c, wt in enumerate(ws))
        o_ref[rows, :] = (num / sum(ws) * _silu(g_ref[rows, :])).astype(BF16)
        return carry

    lax.fori_loop(0, S // rows_per_step, merge, 0)


def _dil(u, cd, sd, qg, kg, B, S):
    col = lambda c: (lambda b, h: (b, c // LANE + h))
    const = lambda b, h: (0, 0)
    blk = lambda im: pl.BlockSpec((S, LANE), im)
    kv_rows = max(S + d * DIL_BLK for _, d in DIL_CONFIGS)
    table = pl.BlockSpec((S, LANE), lambda b, h: (b, 0), pipeline_mode=pl.Buffered(1))
    return pl.pallas_call(
        functools.partial(_dil_kernel, S=S),
        grid=(B, DIL_H),
        in_specs=[blk(col(COL_DQ)), blk(col(COL_DK)), blk(col(COL_DV)), blk(col(COL_GD)),
                  table, table,
                  pl.BlockSpec((1, DIL_HD), const), pl.BlockSpec((1, DIL_HD), const)],
        out_specs=blk(lambda b, h: (b, h)),
        out_shape=jax.ShapeDtypeStruct((B * S, DIL_H * DIL_HD), BF16),
        scratch_shapes=[pltpu.VMEM((S, DIL_HD), F32), pltpu.VMEM((S, DIL_HD), F32),
                        pltpu.VMEM((S, DIL_HD), BF16),
                        pltpu.VMEM((kv_rows, DIL_HD), BF16), pltpu.VMEM((kv_rows, DIL_HD), BF16),
                        pltpu.VMEM((len(DIL_CONFIGS), S, DIL_HD), F32),
                        pltpu.VMEM((len(DIL_CONFIGS), S, LANE), F32),
                        pltpu.VMEM((S // DIL_DIRECT_STRIDE, DIL_HD), F32)],
        compiler_params=_cparams(("parallel", "parallel")),
        name="dilated",
    )(u, u, u, u, cd, sd, qg, kg)


SRC_ALR, SRC_XBC, SRC_DT, SRC_CQKV, SRC_KR, SRC_DQKV = 3072, 3088, 4112, 4120, 4632, 4696
N_IN = 6232
PACK_RUNS = ((0, COL_XBC, 0), (SRC_XBC, SSD_CONV_DIM, COL_XBC), (SRC_CQKV, 512, COL_CQKV),
             (SRC_DQKV, 3 * 512, COL_DQ))
PACK_PIECE = 512


def _pack_kernel(wt_ref, o_ref):
    tc = wt_ref.shape[2]
    for src, width, dst in PACK_RUNS:
        for c0 in range(0, width, PACK_PIECE):
            o_ref[0, :, dst + c0:dst + c0 + PACK_PIECE] = (
                wt_ref[0, src + c0:src + c0 + PACK_PIECE, :].T.astype(BF16))
    small = jnp.concatenate(
        [wt_ref[0, SRC_KR:SRC_KR + MLA_ROPE, :], wt_ref[0, SRC_ALR:SRC_ALR + GLA_GATE_RANK, :],
         wt_ref[0, SRC_DT:SRC_DT + SSD_H, :], jnp.zeros((LANE - SM_DT - SSD_H, tc), F32)], axis=0)
    o_ref[0, :, COL_SMALL:COL_SMALL + LANE] = small.T.astype(BF16)


def _pack_w_in(w, tc=256):
    depth, kdim, n_in = w.shape
    assert n_in == N_IN and kdim % tc == 0 and all(width % PACK_PIECE == 0 for _, width, _ in PACK_RUNS)
    return pl.pallas_call(
        _pack_kernel,
        grid=(depth, kdim // tc),
        in_specs=[pl.BlockSpec((1, N_IN, tc), lambda l, i: (l, 0, i))],
        out_specs=pl.BlockSpec((1, tc, N_PACK), lambda l, i: (l, i, 0)),
        out_shape=jax.ShapeDtypeStruct((depth, kdim, N_PACK), BF16),
        compiler_params=_cparams(("parallel", "parallel")),
        name="pack_w_in",
    )(jnp.swapaxes(w, 1, 2))


def _row(v, width=None, offset=0):
    v = v.astype(F32).reshape(1, -1)
    if width is None:
        return v
    return jnp.pad(v, ((0, 0), (offset, width - offset - v.shape[1])))


def _pad_heads(w, n_heads, real, padded):
    k = w.shape[0]
    w = w.reshape(k, n_heads, real)
    return jnp.pad(w, ((0, 0), (0, 0), (0, padded - real))).reshape(k, n_heads * padded)


def kernel(x, positions, ln_g, w_in, w_out, gla_gate_w2, gla_gate_b, gla_norm_g, ssd_conv_w, ssd_conv_b,
           ssd_dt_bias, ssd_A_log, ssd_D, ssd_norm_g, mla_q_norm_g, mla_kv_norm_g, mla_w_uq, mla_w_ukv,
           mla_q_head_g, mla_k_head_g, dil_q_g, dil_k_g):
    B, S, D = x.shape
    depth = w_in.shape[0]
    assert D == D_MODEL and S % SSD_CHUNK == 0 and S % GLA_ROWS == 0
    assert all(S % (d * DIL_BLK) == 0 for _, d in DIL_CONFIGS)
    T = B * S
    cm, s1, s2, cd, sd = _rope_tables(positions)
    w_in_packed = _pack_w_in(w_in)
    w_out_bf16 = w_out.astype(BF16)
    xf = x.reshape(T, D)
    for l in range(depth):
        u = _inproj(xf, _row(ln_g[l]), w_in_packed, l)
        w2p = jnp.pad(gla_gate_w2[l], ((SM_LR, LANE - SM_LR - GLA_GATE_RANK), (0, 0))).astype(BF16)
        ya = _gla(u, w2p, _row(gla_gate_b[l]), _row(gla_norm_g[l]), B, S)
        yb = _ssd(u, ssd_conv_w[l].astype(F32), _row(ssd_conv_b[l]),
                  _row(ssd_dt_bias[l], LANE, SM_DT), _row(ssd_A_log[l], LANE, SM_DT),
                  _row(jnp.repeat(ssd_D[l], SSD_HD)), _row(ssd_norm_g[l]), B, S)
        q, k, v = _mla_prep(u, cm, s1, s2,
                            _pad_heads(mla_w_uq[l], MLA_H, MLA_QK, 256).astype(BF16),
                            mla_w_ukv[l].astype(BF16), _row(mla_q_norm_g[l]), _row(mla_kv_norm_g[l]),
                            _row(mla_q_head_g[l], 256), _row(mla_k_head_g[l], 256), B, S)
        yc = _flash(q, k, v, u, B, S)
        yd = _dil(u, cd, sd, _row(dil_q_g[l]), _row(dil_k_g[l]), B, S)
        xf = _outproj(xf, ya, yb, yc, yd, w_out_bf16, l)
    return xf.reshape(B, S, D)
```

```python
import functools
import math

import jax
import jax.numpy as jnp
import numpy as np
from jax import lax
from jax.experimental import pallas as pl
from jax.experimental.pallas import tpu as pltpu

F32 = jnp.float32
BF16 = jnp.bfloat16

D_MODEL = 2048
GLA_H, GLA_DK, GLA_DV = 4, 64, 128
GLA_GATE_RANK = 16
GLA_GATE_TAU = 16.0
SSD_DINNER, SSD_HD, SSD_G, SSD_N, SSD_CONV = 512, 64, 2, 128, 4
SSD_H = SSD_DINNER // SSD_HD
SSD_CONV_DIM = SSD_DINNER + 2 * SSD_G * SSD_N
MLA_H, MLA_NOPE, MLA_ROPE, MLA_V = 4, 128, 64, 128
MLA_QK = MLA_NOPE + MLA_ROPE
MLA_Q_LORA, MLA_KV_LORA = 384, 128
DIL_H, DIL_HD = 4, 128
DIL_CONFIGS = ((128, 1), (512, 4), (2048, 16))
ROPE_THETA = 10000.0
EPS = 1e-6
LOG2E = math.log2(math.e)
D_MIX = GLA_H * GLA_DV + SSD_DINNER + MLA_H * MLA_V + DIL_H * DIL_HD

LANE = 128
VMEM_LIMIT = 52 * 1024 * 1024

COL_GA, COL_GB, COL_GC, COL_GD = 0, 512, 1024, 1536
COL_AQK = 2048
COL_AV = 2560
COL_XBC = 3072
COL_CQKV = 4096
COL_DQ, COL_DK, COL_DV = 4608, 5120, 5632
COL_SMALL = 6144
COL_END = COL_SMALL + LANE
MXU_DIM = 256
N_PACK = 6400
SM_KR, SM_LR, SM_DT = 0, 64, 80

GLA_CHUNK = 32
GLA_ROWS = 1024
GLA_BLOCK = 512
SSD_CHUNK = 256
SSD_ROWS = 512
DIL_BLK = 128
DIL_UNROLL = 16
DIL_DIRECT_STRIDE = 4


def _cparams(sem):
    return pltpu.CompilerParams(dimension_semantics=sem, vmem_limit_bytes=VMEM_LIMIT)


def _silu(x):
    return x * (1.0 / (1.0 + jnp.exp(-x)))


def _dot(a, b):
    return jnp.dot(a, b, preferred_element_type=F32)


def _dot_nt(a, b):
    return lax.dot_general(a, b, (((1,), (1,)), ((), ())), preferred_element_type=F32)


def _dot_tn(a, b):
    return lax.dot_general(a, b, (((0,), (0,)), ((), ())), preferred_element_type=F32)


def _split_dot(tri, x):
    hi = x.astype(BF16)
    lo = (x - hi.astype(F32)).astype(BF16)
    return _dot(tri, hi) + _dot(tri, lo)


def _rope_tables_kernel(pos_ref, f_ref, cm_ref, s1_ref, s2_ref, cd_ref, sd_ref):
    half = LANE // 2
    ang = pos_ref[...] * f_ref[...]
    cos, sin = jnp.cos(ang), jnp.sin(ang)
    lane = lax.broadcasted_iota(jnp.int32, (1, LANE), 1)
    low = lane < half
    cm_ref[...] = jnp.where(low, cos, 0.0)
    s1_ref[...] = jnp.where(lane < MLA_ROPE // 2, -sin, 0.0)
    s2_ref[...] = jnp.where((lane >= MLA_ROPE // 2) & low, sin, 0.0)
    cd_ref[...] = jnp.where(low, pltpu.roll(cos, half, 1), cos)
    sd_ref[...] = jnp.where(low, -pltpu.roll(sin, half, 1), sin)


def _rope_tables(positions):
    assert MLA_ROPE == LANE // 2 and DIL_HD == LANE
    T = positions.size
    ts = min(T, 2048)
    pos = positions.reshape(T, 1).astype(F32)
    lane = np.arange(LANE // 2)
    fm = np.exp(-math.log(ROPE_THETA) * (lane % (MLA_ROPE // 2)) * (2.0 / MLA_ROPE))
    fd = np.exp(-math.log(ROPE_THETA) * lane * (2.0 / DIL_HD))
    freqs = jnp.asarray(np.concatenate([fm, fd]), F32).reshape(1, LANE)
    row = pl.BlockSpec((ts, LANE), lambda i: (i, 0))
    return pl.pallas_call(
        _rope_tables_kernel,
        grid=(T // ts,),
        in_specs=[pl.BlockSpec((ts, 1), lambda i: (i, 0)), pl.BlockSpec((1, LANE), lambda i: (0, 0))],
        out_specs=[row] * 5,
        out_shape=[jax.ShapeDtypeStruct((T, LANE), F32)] * 5,
        compiler_params=_cparams(("parallel",)),
        name="rope_tables",
    )(pos, freqs)


def _inproj_kernel(x_ref, g_ref, w_ref, o_ref, h_ref):
    @pl.when(pl.program_id(1) == 0)
    def _():
        x = x_ref[...]
        ms = jnp.mean(x * x, axis=-1, keepdims=True)
        h_ref[...] = (x * lax.rsqrt(ms + EPS) * g_ref[...]).astype(BF16)

    o_ref[...] = _dot(h_ref[...], w_ref[...])


def _inproj(x, g, w, layer, tm=1024, tn=5 * MXU_DIM):
    T = x.shape[0]
    tm = min(tm, T)
    return pl.pallas_call(
        _inproj_kernel,
        grid=(T // tm, N_PACK // tn),
        in_specs=[
            pl.BlockSpec((tm, D_MODEL), lambda i, j: (i, 0)),
            pl.BlockSpec((1, D_MODEL), lambda i, j: (0, 0)),
            pl.BlockSpec((None, D_MODEL, tn), lambda i, j: (layer, 0, j)),
        ],
        out_specs=pl.BlockSpec((tm, tn), lambda i, j: (i, j)),
        out_shape=jax.ShapeDtypeStruct((T, N_PACK), F32),
        scratch_shapes=[pltpu.VMEM((tm, D_MODEL), BF16)],
        compiler_params=_cparams(("parallel", "arbitrary")),
        name="inproj",
    )(x, g, w)


def _outproj_kernel(x_ref, ya_ref, yb_ref, yc_ref, yd_ref, w_ref, o_ref):
    acc = x_ref[...]
    for n, y_ref in enumerate((ya_ref, yb_ref, yc_ref, yd_ref)):
        acc = acc + _dot(y_ref[...], w_ref[n * 512:(n + 1) * 512, :])
    o_ref[...] = acc


def _outproj(x, ya, yb, yc, yd, w, layer, tm=256):
    T = x.shape[0]
    tm = min(tm, T)
    yspec = pl.BlockSpec((tm, 512), lambda i: (i, 0))
    return pl.pallas_call(
        _outproj_kernel,
        grid=(T // tm,),
        in_specs=[pl.BlockSpec((tm, D_MODEL), lambda i: (i, 0)), yspec, yspec, yspec, yspec,
                  pl.BlockSpec((None, D_MIX, D_MODEL), lambda i: (layer, 0, 0))],
        out_specs=pl.BlockSpec((tm, D_MODEL), lambda i: (i, 0)),
        out_shape=jax.ShapeDtypeStruct((T, D_MODEL), F32),
        compiler_params=_cparams(("parallel",)),
        name="outproj",
    )(x, ya, yb, yc, yd, w)


def _gla_kernel(qk_ref, v_ref, gate_ref, sm_ref, w2_ref, b2_ref, ng_ref, o_ref, st_ref):
    @pl.when(pl.program_id(1) == 0)
    def _():
        st_ref[...] = jnp.zeros_like(st_ref)

    R, RB, C = GLA_ROWS, GLA_BLOCK, GLA_CHUNK
    xg = _dot(sm_ref[...].astype(BF16), w2_ref[...]) + b2_ref[...]
    logd = (jnp.minimum(xg, 0.0) - jnp.log(1.0 + jnp.exp(-jnp.abs(xg)))) * (1.0 / GLA_GATE_TAU)
    ri = lax.broadcasted_iota(jnp.int32, (RB, RB), 0)
    ci = lax.broadcasted_iota(jnp.int32, (RB, RB), 1)
    back = ri - ci
    in_chunk = (back >= 0) & (back <= (ri & (C - 1)))
    cum = jnp.where(in_chunk, 1.0, 0.0).astype(BF16)
    ng = ng_ref[...]
    n_chunks = RB // C
    hk = GLA_H * GLA_DK
    wide = n_chunks * GLA_DK
    place = ((lax.broadcasted_iota(jnp.int32, (RB, wide), 0) >> (C.bit_length() - 1))
             == (lax.broadcasted_iota(jnp.int32, (RB, wide), 1) >> (GLA_DK.bit_length() - 1)))

    def block_diag(x):
        pair = jnp.concatenate([x, x], axis=1)
        return jnp.where(place, jnp.concatenate([pair] * (n_chunks // 2), axis=1), jnp.zeros((), x.dtype))

    for sb in range(R // RB):
        rows = slice(sb * RB, (sb + 1) * RB)
        bc = _split_dot(cum, logd[rows]).reshape(n_chunks, C, hk)
        b_mid = bc[:, C // 2 - 1:C // 2, :]
        b_last = bc[:, C - 1:C, :]
        q = (qk_ref[rows, :hk] * (GLA_DK ** -0.5)).reshape(n_chunks, C, hk)
        k = qk_ref[rows, hk:].reshape(n_chunks, C, hk)
        qd = (q * jnp.exp(bc - b_mid)).astype(BF16).reshape(RB, hk)
        kd = (k * jnp.exp(b_mid - bc)).astype(BF16).reshape(RB, hk)
        qe = (q * jnp.exp(bc)).astype(BF16).reshape(RB, hk)
        kl = (k * jnp.exp(b_last - bc)).astype(BF16).reshape(RB, hk)
        e_last = jnp.exp(b_last)
        for h in range(GLA_H):
            kcols = slice(h * GLA_DK, (h + 1) * GLA_DK)
            vcols = slice(h * GLA_DV, (h + 1) * GLA_DV)
            v = v_ref[rows, vcols].astype(BF16)
            attn = jnp.where(in_chunk, _dot_nt(qd[:, kcols], kd[:, kcols]), 0.0).astype(BF16)
            u_all = _dot_tn(v, block_diag(kl[:, kcols]))
            st = st_ref[h]
            entering = []
            for c in range(n_chunks):
                entering.append(st)
                st = st * e_last[c][:, kcols] + u_all[:, c * GLA_DK:(c + 1) * GLA_DK]
            st_ref[h] = st
            s_all = jnp.concatenate(entering, axis=1).astype(BF16)
            o = _dot(attn, v) + _dot_nt(block_diag(qe[:, kcols]), s_all)
            ms = jnp.mean(o * o, axis=-1, keepdims=True)
            y = o * lax.rsqrt(ms + EPS) * ng * _silu(gate_ref[rows, vcols])
            o_ref[rows, vcols] = y.astype(BF16)


def _gla(u, w2p, b2, ng, B, S):
    R = GLA_ROWS
    nb = S // R
    row = lambda c: (lambda b, i: (b * nb + i, c))
    const = lambda b, i: (0, 0)
    return pl.pallas_call(
        _gla_kernel,
        grid=(B, nb),
        in_specs=[
            pl.BlockSpec((R, 512), row(COL_AQK // 512)),
            pl.BlockSpec((R, 512), row(COL_AV // 512)),
            pl.BlockSpec((R, 512), row(COL_GA // 512)),
            pl.BlockSpec((R, LANE), row(COL_SMALL // LANE)),
            pl.BlockSpec((LANE, GLA_H * GLA_DK), const),
            pl.BlockSpec((1, GLA_H * GLA_DK), const),
            pl.BlockSpec((1, GLA_DV), const),
        ],
        out_specs=pl.BlockSpec((R, 512), row(0)),
        out_shape=jax.ShapeDtypeStruct((B * S, GLA_H * GLA_DV), BF16),
        scratch_shapes=[pltpu.VMEM((GLA_H, GLA_DV, GLA_DK), F32)],
        compiler_params=_cparams(("parallel", "arbitrary")),
        name="gla",
    )(u, u, u, u, w2p, b2, ng)


def _ssd_kernel(xbc_ref, z_ref, sm_ref, cw_ref, cb_ref, dtb_ref, a_ref, d_ref, ng_ref, o_ref,
                xpad_ref, st_ref, y_ref):
    R, L = SSD_ROWS, SSD_CHUNK
    HALO = 8

    @pl.when(pl.program_id(1) == 0)
    def _():
        st_ref[...] = jnp.zeros_like(st_ref)
        xpad_ref[...] = jnp.zeros((HALO, SSD_CONV_DIM), F32)

    x = xbc_ref[...]
    halo = xpad_ref[...]
    row = lax.broadcasted_iota(jnp.int32, (HALO, 1), 0)
    conv = cb_ref[...] + x * cw_ref[SSD_CONV - 1:SSD_CONV, :]
    for k in range(1, SSD_CONV):
        shifted = pltpu.roll(x, k, 0)
        head = jnp.where(row < k, pltpu.roll(halo, k, 0), shifted[0:HALO])
        conv = conv + jnp.concatenate([head, shifted[HALO:]], axis=0) * cw_ref[SSD_CONV - 1 - k:SSD_CONV - k, :]
    xpad_ref[...] = x[R - HALO:R]
    xact = _silu(conv)

    lane = lax.broadcasted_iota(jnp.int32, (1, LANE), 1)
    is_dt = (lane >= SM_DT) & (lane < SM_DT + SSD_H)
    pre = sm_ref[...] + dtb_ref[...]
    dt_all = jnp.maximum(pre, 0.0) + jnp.log(1.0 + jnp.exp(-jnp.abs(pre)))
    a_all = jnp.where(is_dt, dt_all * -jnp.exp(a_ref[...]), 0.0)
    ri = lax.broadcasted_iota(jnp.int32, (L, L), 0)
    ci = lax.broadcasted_iota(jnp.int32, (L, L), 1)
    lower = ci <= ri
    cum = jnp.where(lower, 1.0, 0.0).astype(BF16)

    heads_per_group = SSD_H // SSD_G
    for sb in range(R // L):
        rows = slice(sb * L, (sb + 1) * L)
        cs_all = _split_dot(cum, a_all[rows])
        cs_t = cs_all.T
        for g in range(SSD_G):
            bm = xact[rows, SSD_DINNER + g * SSD_N:SSD_DINNER + (g + 1) * SSD_N].astype(BF16)
            cm = xact[rows, SSD_DINNER + SSD_G * SSD_N + g * SSD_N:
                      SSD_DINNER + SSD_G * SSD_N + (g + 1) * SSD_N].astype(BF16)
            scores = _dot_nt(cm, bm)
            for hh in range(heads_per_group):
                h = g * heads_per_group + hh
                hcols = slice(h * SSD_HD, (h + 1) * SSD_HD)
                xs = xact[rows, hcols]
                dt = dt_all[rows, SM_DT + h:SM_DT + h + 1]
                cs_col = cs_all[:, SM_DT + h:SM_DT + h + 1]
                cs_row = cs_t[SM_DT + h:SM_DT + h + 1, :]
                cs_last = cs_col[L - 1:L, :]
                decay = jnp.where(lower, jnp.exp(cs_col - cs_row), 0.0)
                xdt = xs * dt
                y = _dot((scores * decay).astype(BF16), xdt.astype(BF16))
                prev = st_ref[h]
                y = y + _dot_nt(cm, prev.astype(BF16)) * jnp.exp(cs_col)
                st_ref[h] = prev * jnp.exp(cs_last) + _dot_tn(
                    (xdt * jnp.exp(cs_last - cs_col)).astype(BF16), bm)
                y_ref[rows, hcols] = y + xs * d_ref[:, hcols]

    yz = y_ref[...] * _silu(z_ref[...])
    gw = SSD_DINNER // SSD_G
    for g in range(SSD_G):
        yg = yz[:, g * gw:(g + 1) * gw]
        ms = jnp.mean(yg * yg, axis=-1, keepdims=True)
        o_ref[:, g * gw:(g + 1) * gw] = (yg * lax.rsqrt(ms + EPS) * ng_ref[:, g * gw:(g + 1) * gw]).astype(BF16)


def _ssd(u, cw, cb, dtb, a_pad, d_exp, ng, B, S):
    L = SSD_ROWS
    nb = S // L
    row = lambda c: (lambda b, i: (b * nb + i, c))
    const = lambda b, i: (0, 0)
    return pl.pallas_call(
        _ssd_kernel,
        grid=(B, nb),
        in_specs=[
            pl.BlockSpec((L, SSD_CONV_DIM), row(COL_XBC // SSD_CONV_DIM)),
            pl.BlockSpec((L, 512), row(COL_GB // 512)),
            pl.BlockSpec((L, LANE), row(COL_SMALL // LANE)),
            pl.BlockSpec((SSD_CONV, SSD_CONV_DIM), const),
            pl.BlockSpec((1, SSD_CONV_DIM), const),
            pl.BlockSpec((1, LANE), const),
            pl.BlockSpec((1, LANE), const),
            pl.BlockSpec((1, SSD_DINNER), const),
            pl.BlockSpec((1, SSD_DINNER), const),
        ],
        out_specs=pl.BlockSpec((L, SSD_DINNER), row(0)),
        out_shape=jax.ShapeDtypeStruct((B * S, SSD_DINNER), BF16),
        scratch_shapes=[pltpu.VMEM((8, SSD_CONV_DIM), F32),
                        pltpu.VMEM((SSD_H, SSD_HD, SSD_N), F32),
                        pltpu.VMEM((L, SSD_DINNER), F32)],
        compiler_params=_cparams(("parallel", "arbitrary")),
        name="ssd",
    )(u, u, u, cw, cb, dtb, a_pad, d_exp, ng)


def _rope64(x, cm, s1, s2):
    return x * cm + pltpu.roll(x, LANE - MLA_ROPE // 2, 1) * s1 + pltpu.roll(x, MLA_ROPE // 2, 1) * s2


def _mla_prep_kernel(c_ref, sm_ref, cm_ref, s1_ref, s2_ref, wq_ref, wkv_ref, qng_ref, kvng_ref,
                     qhg_ref, khg_ref, q_ref, k_ref, v_ref):
    blk = c_ref[...]
    cq = blk[:, :MLA_Q_LORA]
    ckv = blk[:, MLA_Q_LORA:]
    def lane_tile_sum(x):
        tiles = [x[:, c:c + LANE] for c in range(0, x.shape[1], LANE)]
        return jnp.sum(functools.reduce(lambda a, b: a + b, tiles), axis=-1, keepdims=True)

    cqn = cq * lax.rsqrt(lane_tile_sum(cq * cq) * (1.0 / MLA_Q_LORA) + EPS) * qng_ref[...]
    ckvn = ckv * lax.rsqrt(lane_tile_sum(ckv * ckv) * (1.0 / MLA_KV_LORA) + EPS) * kvng_ref[...]
    qf = _dot(cqn.astype(BF16), wq_ref[...])
    kvf = _dot(ckvn.astype(BF16), wkv_ref[...])
    cm, s1, s2 = cm_ref[...], s1_ref[...], s2_ref[...]
    lane = lax.broadcasted_iota(jnp.int32, (1, LANE), 1)
    kr = jnp.where(lane < MLA_ROPE, sm_ref[...], 0.0)
    kr_sq = kr * kr
    qhg, khg = qhg_ref[...], khg_ref[...]
    kr_rot = _rope64(kr * khg[:, LANE:], cm, s1, s2)
    for h in range(MLA_H):
        qh = qf[:, 256 * h:256 * (h + 1)]
        rq = lax.rsqrt(lane_tile_sum(qh * qh) * (1.0 / MLA_QK) + EPS)
        rq = rq * (MLA_QK ** -0.5 * LOG2E)
        q_ref[0, h, :, 0:LANE] = (qh[:, :LANE] * rq * qhg[:, :LANE]).astype(BF16)
        q_ref[0, h, :, LANE:2 * LANE] = _rope64(qh[:, LANE:] * rq * qhg[:, LANE:], cm, s1, s2).astype(BF16)
        kn = kvf[:, 256 * h:256 * h + LANE]
        rk = lax.rsqrt(jnp.sum(kn * kn + kr_sq, axis=-1, keepdims=True) * (1.0 / MLA_QK) + EPS)
        k_ref[0, h, :, 0:LANE] = (kn * rk * khg[:, :LANE]).astype(BF16)
        k_ref[0, h, :, LANE:2 * LANE] = (kr_rot * rk).astype(BF16)
        v_ref[0, h, :, :] = kvf[:, 256 * h + LANE:256 * (h + 1)].astype(BF16)


def _mla_prep(u, cm, s1, s2, wq, wkv, qng, kvng, qhg, khg, B, S, ts=512):
    ts = min(ts, S)
    nb = S // ts
    row = lambda c: (lambda b, i: (b * nb + i, c))
    const = lambda b, i: (0, 0)
    hspec = lambda w: pl.BlockSpec((1, MLA_H, ts, w), lambda b, i: (b, 0, i, 0))
    return pl.pallas_call(
        _mla_prep_kernel,
        grid=(B, nb),
        in_specs=[
            pl.BlockSpec((ts, 512), row(COL_CQKV // 512)),
            pl.BlockSpec((ts, LANE), row(COL_SMALL // LANE)),
            pl.BlockSpec((ts, LANE), row(0)), pl.BlockSpec((ts, LANE), row(0)), pl.BlockSpec((ts, LANE), row(0)),
            pl.BlockSpec((MLA_Q_LORA, MLA_H * 256), const),
            pl.BlockSpec((MLA_KV_LORA, MLA_H * 256), const),
            pl.BlockSpec((1, MLA_Q_LORA), const),
            pl.BlockSpec((1, MLA_KV_LORA), const),
            pl.BlockSpec((1, 256), const),
            pl.BlockSpec((1, 256), const),
        ],
        out_specs=[hspec(256), hspec(256), hspec(MLA_V)],
        out_shape=[jax.ShapeDtypeStruct((B, MLA_H, S, 256), BF16),
                   jax.ShapeDtypeStruct((B, MLA_H, S, 256), BF16),
                   jax.ShapeDtypeStruct((B, MLA_H, S, MLA_V), BF16)],
        compiler_params=_cparams(("parallel", "parallel")),
        name="mla_prep",
    )(u, u, cm, s1, s2, wq, wkv, qng, kvng, qhg, khg)


def _flash_kernel(q_ref, k_ref, v_ref, g_ref, o_ref, m_ref, l_ref, acc_ref, sa_ref, sb_ref, *, t):
    i = pl.program_id(2)
    m_ref[...] = jnp.full_like(m_ref, -jnp.inf)
    l_ref[...] = jnp.zeros_like(l_ref)
    acc_ref[...] = jnp.zeros_like(acc_ref)
    q = q_ref[0, 0]

    def rows(j):
        return pl.ds(pl.multiple_of(j * t, t), t)

    def scores(j):
        return _dot_nt(q, k_ref[0, 0, rows(j), :])

    def update(s, j, qrows=slice(0, t)):
        nk = s.shape[1]
        m_old = m_ref[qrows]
        m_new = jnp.maximum(m_old, jnp.max(s, axis=-1, keepdims=True))
        alpha = jnp.exp2(m_old - m_new)
        p = jnp.exp2(s - jnp.concatenate([m_new] * (nk // LANE), axis=1))
        l_ref[qrows] = alpha * l_ref[qrows] + jnp.sum(p, axis=-1, keepdims=True)
        vrows = pl.ds(pl.multiple_of(j * t, t), nk)
        acc_ref[qrows] = alpha * acc_ref[qrows] + _dot(p.astype(BF16), v_ref[0, 0, vrows, :])
        m_ref[qrows] = m_new

    def diagonal(s_ref):
        h = t // 2
        top = lax.broadcasted_iota(jnp.int32, (h, h), 1) <= lax.broadcasted_iota(jnp.int32, (h, h), 0)
        update(jnp.where(top, s_ref[0:h, 0:h], -jnp.inf), i, slice(0, h))
        low = lax.broadcasted_iota(jnp.int32, (h, t), 1) <= lax.broadcasted_iota(jnp.int32, (h, t), 0) + h
        update(jnp.where(low, s_ref[h:t, :], -jnp.inf), i, slice(h, t))

    sa_ref[...] = scores(0)

    def pair(p, carry):
        j = 2 * p
        s = sa_ref[...]
        sb_ref[...] = scores(j + 1)
        update(s, j)
        s = sb_ref[...]
        sa_ref[...] = scores(j + 2)
        update(s, j + 1)
        return carry

    lax.fori_loop(0, i // 2, pair, 0)

    @pl.when(i % 2 == 1)
    def _():
        s = sa_ref[...]
        sb_ref[...] = scores(i)
        update(s, i - 1)
        diagonal(sb_ref)

    @pl.when(i % 2 == 0)
    def _():
        diagonal(sa_ref)

    o_ref[...] = (acc_ref[...] / l_ref[...] * _silu(g_ref[...])).astype(BF16)


def _flash(q, k, v, u, B, S, t=1024):
    t = min(t, S)
    nq = S // t
    return pl.pallas_call(
        functools.partial(_flash_kernel, t=t),
        grid=(B, MLA_H, nq),
        in_specs=[
            pl.BlockSpec((1, 1, t, 256), lambda b, h, i: (b, h, i, 0)),
            pl.BlockSpec((1, 1, S, 256), lambda b, h, i: (b, h, 0, 0)),
            pl.BlockSpec((1, 1, S, MLA_V), lambda b, h, i: (b, h, 0, 0)),
            pl.BlockSpec((t, LANE), lambda b, h, i: (b * nq + i, COL_GC // LANE + h)),
        ],
        out_specs=pl.BlockSpec((t, LANE), lambda b, h, i: (b * nq + i, h)),
        out_shape=jax.ShapeDtypeStruct((B * S, MLA_H * MLA_V), BF16),
        scratch_shapes=[pltpu.VMEM((t, LANE), F32), pltpu.VMEM((t, LANE), F32), pltpu.VMEM((t, MLA_V), F32),
                        pltpu.VMEM((t, t), F32), pltpu.VMEM((t, t), F32)],
        compiler_params=_cparams(("parallel", "parallel", "arbitrary")),
        name="mla_flash",
    )(q, k, v, u)


def _dil_kernel(q_ref, k_ref, v_ref, g_ref, cd_ref, sd_ref, qg_ref, kg_ref, o_ref,
                qs_ref, ks_ref, qd_ref, kd_ref, vd_ref, oc_ref, ec_ref, tmp_ref, *, S):
    Q = DIL_BLK
    cd, sd = cd_ref[...], sd_ref[...]

    def norm_rope(x, g):
        xn = x * lax.rsqrt(jnp.mean(x * x, axis=-1, keepdims=True) + EPS) * g
        return xn * cd + pltpu.roll(xn, DIL_HD // 2, 1) * sd

    qs_ref[...] = norm_rope(q_ref[...], qg_ref[...] * (DIL_HD ** -0.5 * LOG2E))
    ks_ref[...] = norm_rope(k_ref[...], kg_ref[...])

    ra = lax.broadcasted_iota(jnp.int32, (Q, 2 * Q), 0)
    cc = lax.broadcasted_iota(jnp.int32, (Q, 2 * Q), 1)
    band = (cc >= ra) & (cc <= ra + Q)
    bias_inner = jnp.where(band, 0.0, -jnp.inf)
    bias_first = jnp.where(band & (cc >= Q), 0.0, -jnp.inf)
    n_units = S // Q

    for c, (w, d) in enumerate(DIL_CONFIGS):
        assert w // d == Q
        n_sub = S // d
        pitch = n_sub + Q
        nblk = n_sub // Q
        piece = min(n_sub, 512)
        streams = ((qs_ref, qd_ref, n_sub, 0), (ks_ref, kd_ref, pitch, Q), (v_ref, vd_ref, pitch, Q))
        for r in range(d):
            kd_ref[r * pitch:r * pitch + Q, :] = jnp.zeros((Q, DIL_HD), BF16)
            vd_ref[r * pitch:r * pitch + Q, :] = jnp.zeros((Q, DIL_HD), BF16)
        if d <= DIL_DIRECT_STRIDE:
            for r in range(d):
                for c0 in range(0, n_sub, piece):
                    src = pl.ds(r + c0 * d, piece, stride=d)
                    for src_ref, dst_ref, cpitch, lead in streams:
                        dst = r * cpitch + lead + c0
                        dst_ref[dst:dst + piece, :] = src_ref[src, :].astype(BF16)
        else:
            outer, inner = DIL_DIRECT_STRIDE, d // DIL_DIRECT_STRIDE
            n_outer = S // outer
            assert d % outer == 0 and inner <= DIL_DIRECT_STRIDE
            for r0 in range(outer):
                for src_ref, dst_ref, cpitch, lead in streams:
                    step = min(n_outer, 512)
                    for c0 in range(0, n_outer, step):
                        tmp_ref[c0:c0 + step, :] = src_ref[pl.ds(r0 + c0 * outer, step, stride=outer), :]
                    for m in range(inner):
                        dst = (r0 + outer * m) * cpitch + lead
                        dst_ref[dst:dst + n_sub, :] = tmp_ref[pl.ds(m, n_sub, stride=inner), :].astype(BF16)

        unroll = min(DIL_UNROLL, n_units)

        def group(n0, carry, c=c, n_sub=n_sub, pitch=pitch, nblk=nblk, unroll=unroll):
            for uu in range(unroll):
                n = n0 * unroll + uu
                r = n // nblk
                i = n - r * nblk
                qrows = pl.ds(pl.multiple_of(r * n_sub + i * Q, Q), Q)
                krows = pl.ds(pl.multiple_of(r * pitch + i * Q, Q), 2 * Q)
                s = _dot_nt(qd_ref[qrows, :], kd_ref[krows, :]) + jnp.where(i > 0, bias_inner, bias_first)
                m = jnp.max(s, axis=-1, keepdims=True)
                p = jnp.exp2(s - m)
                l = jnp.sum(p, axis=-1, keepdims=True)
                nat = pl.ds(i * (d * Q) + r, Q, stride=d) if d > 1 else pl.ds(pl.multiple_of(i * Q, Q), Q)
                oc_ref[c, nat, :] = _dot(p.astype(BF16), vd_ref[krows, :]) * (1.0 / l)
                ec_ref[c, nat, :] = jnp.broadcast_to(m + jnp.log2(l), (Q, LANE))
            return carry

        lax.fori_loop(0, n_units // unroll, group, 0)

    rows_per_step = min(S, 256)

    def merge(n, carry):
        rows = pl.ds(pl.multiple_of(n * rows_per_step, rows_per_step), rows_per_step)
        es = [ec_ref[c, rows, :] for c in range(len(DIL_CONFIGS))]
        e_max = functools.reduce(jnp.maximum, es)
        ws = [jnp.exp2(e - e_max) for e in es]
        num = sum(wt * oc_ref[c, rows, :] for c, wt in enumerate(ws))
        o_ref[rows, :] = (num / sum(ws) * _silu(g_ref[rows, :])).astype(BF16)
        return carry

    lax.fori_loop(0, S // rows_per_step, merge, 0)


def _dil(u, cd, sd, qg, kg, B, S):
    col = lambda c: (lambda b, h: (b, c // LANE + h))
    const = lambda b, h: (0, 0)
    blk = lambda im: pl.BlockSpec((S, LANE), im)
    kv_rows = max(S + d * DIL_BLK for _, d in DIL_CONFIGS)
    table = pl.BlockSpec((S, LANE), lambda b, h: (b, 0), pipeline_mode=pl.Buffered(1))
    return pl.pallas_call(
        functools.partial(_dil_kernel, S=S),
        grid=(B, DIL_H),
        in_specs=[blk(col(COL_DQ)), blk(col(COL_DK)), blk(col(COL_DV)), blk(col(COL_GD)),
                  table, table,
                  pl.BlockSpec((1, DIL_HD), const), pl.BlockSpec((1, DIL_HD), const)],
        out_specs=blk(lambda b, h: (b, h)),
        out_shape=jax.ShapeDtypeStruct((B * S, DIL_H * DIL_HD), BF16),
        scratch_shapes=[pltpu.VMEM((S, DIL_HD), F32), pltpu.VMEM((S, DIL_HD), F32),
                        pltpu.VMEM((S, DIL_HD), BF16),
                        pltpu.VMEM((kv_rows, DIL_HD), BF16), pltpu.VMEM((kv_rows, DIL_HD), BF16),
                        pltpu.VMEM((len(DIL_CONFIGS), S, DIL_HD), F32),
                        pltpu.VMEM((len(DIL_CONFIGS), S, LANE), F32),
                        pltpu.VMEM((S // DIL_DIRECT_STRIDE, DIL_HD), F32)],
        compiler_params=_cparams(("parallel", "parallel")),
        name="dilated",
    )(u, u, u, u, cd, sd, qg, kg)


SRC_ALR, SRC_XBC, SRC_DT, SRC_CQKV, SRC_KR, SRC_DQKV = 3072, 3088, 4112, 4120, 4632, 4696
N_IN = 6232
PACK_RUNS = ((0, COL_XBC, 0), (SRC_XBC, SSD_CONV_DIM, COL_XBC), (SRC_CQKV, 512, COL_CQKV),
             (SRC_DQKV, 3 * 512, COL_DQ))
PACK_PIECE = 512


def _pack_kernel(wt_ref, o_ref):
    tc = wt_ref.shape[2]
    for src, width, dst in PACK_RUNS:
        for c0 in range(0, width, PACK_PIECE):
            o_ref[0, :, dst + c0:dst + c0 + PACK_PIECE] = (
                wt_ref[0, src + c0:src + c0 + PACK_PIECE, :].T.astype(BF16))
    small = jnp.concatenate(
        [wt_ref[0, SRC_KR:SRC_KR + MLA_ROPE, :], wt_ref[0, SRC_ALR:SRC_ALR + GLA_GATE_RANK, :],
         wt_ref[0, SRC_DT:SRC_DT + SSD_H, :], jnp.zeros((LANE - SM_DT - SSD_H, tc), F32)], axis=0)
    o_ref[0, :, COL_SMALL:COL_END] = small.T.astype(BF16)
    o_ref[0, :, COL_END:N_PACK] = jnp.zeros((tc, N_PACK - COL_END), BF16)


def _pack_w_in(w, tc=256):
    depth, kdim, n_in = w.shape
    assert n_in == N_IN and kdim % tc == 0 and all(width % PACK_PIECE == 0 for _, width, _ in PACK_RUNS)
    return pl.pallas_call(
        _pack_kernel,
        grid=(depth, kdim // tc),
        in_specs=[pl.BlockSpec((1, N_IN, tc), lambda l, i: (l, 0, i))],
        out_specs=pl.BlockSpec((1, tc, N_PACK), lambda l, i: (l, i, 0)),
        out_shape=jax.ShapeDtypeStruct((depth, kdim, N_PACK), BF16),
        compiler_params=_cparams(("parallel", "parallel")),
        name="pack_w_in",
    )(jnp.swapaxes(w, 1, 2))


def _row(v, width=None, offset=0):
    v = v.astype(F32).reshape(1, -1)
    if width is None:
        return v
    return jnp.pad(v, ((0, 0), (offset, width - offset - v.shape[1])))


def _pad_heads(w, n_heads, real, padded):
    k = w.shape[0]
    w = w.reshape(k, n_heads, real)
    return jnp.pad(w, ((0, 0), (0, 0), (0, padded - real))).reshape(k, n_heads * padded)


def kernel(x, positions, ln_g, w_in, w_out, gla_gate_w2, gla_gate_b, gla_norm_g, ssd_conv_w, ssd_conv_b,
           ssd_dt_bias, ssd_A_log, ssd_D, ssd_norm_g, mla_q_norm_g, mla_kv_norm_g, mla_w_uq, mla_w_ukv,
           mla_q_head_g, mla_k_head_g, dil_q_g, dil_k_g):
    B, S, D = x.shape
    depth = w_in.shape[0]
    assert D == D_MODEL and S % SSD_ROWS == 0 and S % GLA_ROWS == 0
    assert all(S % (d * DIL_BLK) == 0 for _, d in DIL_CONFIGS)
    T = B * S
    cm, s1, s2, cd, sd = _rope_tables(positions)
    w_in_packed = _pack_w_in(w_in)
    w_out_bf16 = w_out.astype(BF16)
    xf = x.reshape(T, D)
    for l in range(depth):
        u = _inproj(xf, _row(ln_g[l]), w_in_packed, l)
        w2p = jnp.pad(gla_gate_w2[l], ((SM_LR, LANE - SM_LR - GLA_GATE_RANK), (0, 0))).astype(BF16)
        ya = _gla(u, w2p, _row(gla_gate_b[l]), _row(gla_norm_g[l]), B, S)
        yb = _ssd(u, ssd_conv_w[l].astype(F32), _row(ssd_conv_b[l]),
                  _row(ssd_dt_bias[l], LANE, SM_DT), _row(ssd_A_log[l], LANE, SM_DT),
                  _row(jnp.repeat(ssd_D[l], SSD_HD)), _row(ssd_norm_g[l]), B, S)
        q, k, v = _mla_prep(u, cm, s1, s2,
                            _pad_heads(mla_w_uq[l], MLA_H, MLA_QK, 256).astype(BF16),
                            mla_w_ukv[l].astype(BF16), _row(mla_q_norm_g[l]), _row(mla_kv_norm_g[l]),
                            _row(mla_q_head_g[l], 256), _row(mla_k_head_g[l], 256), B, S)
        yc = _flash(q, k, v, u, B, S)
        yd = _dil(u, cd, sd, _row(dil_q_g[l]), _row(dil_k_g[l]), B, S)
        xf = _outproj(xf, ya, yb, yc, yd, w_out_bf16, l)
    return xf.reshape(B, S, D)
```

```python
import functools
import math

import jax
import jax.numpy as jnp
import numpy as np
from jax import lax
from jax.experimental import pallas as pl
from jax.experimental.pallas import tpu as pltpu

F32 = jnp.float32
BF16 = jnp.bfloat16

D_MODEL = 2048
GLA_H, GLA_DK, GLA_DV = 4, 64, 128
GLA_GATE_RANK = 16
GLA_GATE_TAU = 16.0
SSD_DINNER, SSD_HD, SSD_G, SSD_N, SSD_CONV = 512, 64, 2, 128, 4
SSD_H = SSD_DINNER // SSD_HD
SSD_CONV_DIM = SSD_DINNER + 2 * SSD_G * SSD_N
MLA_H, MLA_NOPE, MLA_ROPE, MLA_V = 4, 128, 64, 128
MLA_QK = MLA_NOPE + MLA_ROPE
MLA_Q_LORA, MLA_KV_LORA = 384, 128
DIL_H, DIL_HD = 4, 128
DIL_CONFIGS = ((128, 1), (512, 4), (2048, 16))
ROPE_THETA = 10000.0
EPS = 1e-6
LOG2E = math.log2(math.e)
D_MIX = GLA_H * GLA_DV + SSD_DINNER + MLA_H * MLA_V + DIL_H * DIL_HD

LANE = 128
VMEM_LIMIT = 52 * 1024 * 1024

COL_GA, COL_GB, COL_GC, COL_GD = 0, 512, 1024, 1536
COL_AQK = 2048
COL_AV = 2560
COL_XBC = 3072
COL_CQKV = 4096
COL_DQ, COL_DK, COL_DV = 4608, 5120, 5632
COL_SMALL = 6144
COL_END = COL_SMALL + LANE
MXU_DIM = 256
N_PACK = 6400
SM_KR, SM_LR, SM_DT = 0, 64, 80

GLA_CHUNK = 32
GLA_ROWS = 1024
GLA_BLOCK = 512
SSD_CHUNK = 256
SSD_ROWS = 512
DIL_BLK = 128
DIL_UNROLL = 16
DIL_DIRECT_STRIDE = 4


def _cparams(sem):
    return pltpu.CompilerParams(dimension_semantics=sem, vmem_limit_bytes=VMEM_LIMIT)


def _silu(x):
    return x * (1.0 / (1.0 + jnp.exp(-x)))


def _dot(a, b):
    return jnp.dot(a, b, preferred_element_type=F32)


def _dot_nt(a, b):
    return lax.dot_general(a, b, (((1,), (1,)), ((), ())), preferred_element_type=F32)


def _dot_tn(a, b):
    return lax.dot_general(a, b, (((0,), (0,)), ((), ())), preferred_element_type=F32)


def _split_dot(tri, x):
    hi = x.astype(BF16)
    lo = (x - hi.astype(F32)).astype(BF16)
    return _dot(tri, hi) + _dot(tri, lo)


def _rope_tables_kernel(pos_ref, f_ref, cm_ref, s1_ref, s2_ref, cd_ref, sd_ref):
    half = LANE // 2
    ang = pos_ref[...] * f_ref[...]
    cos, sin = jnp.cos(ang), jnp.sin(ang)
    lane = lax.broadcasted_iota(jnp.int32, (1, LANE), 1)
    low = lane < half
    cm_ref[...] = jnp.where(low, cos, 0.0)
    s1_ref[...] = jnp.where(lane < MLA_ROPE // 2, -sin, 0.0)
    s2_ref[...] = jnp.where((lane >= MLA_ROPE // 2) & low, sin, 0.0)
    cd_ref[...] = jnp.where(low, pltpu.roll(cos, half, 1), cos)
    sd_ref[...] = jnp.where(low, -pltpu.roll(sin, half, 1), sin)


def _rope_tables(positions):
    assert MLA_ROPE == LANE // 2 and DIL_HD == LANE
    T = positions.size
    ts = min(T, 2048)
    pos = positions.reshape(T, 1).astype(F32)
    lane = np.arange(LANE // 2)
    fm = np.exp(-math.log(ROPE_THETA) * (lane % (MLA_ROPE // 2)) * (2.0 / MLA_ROPE))
    fd = np.exp(-math.log(ROPE_THETA) * lane * (2.0 / DIL_HD))
    freqs = jnp.asarray(np.concatenate([fm, fd]), F32).reshape(1, LANE)
    row = pl.BlockSpec((ts, LANE), lambda i: (i, 0))
    return pl.pallas_call(
        _rope_tables_kernel,
        grid=(T // ts,),
        in_specs=[pl.BlockSpec((ts, 1), lambda i: (i, 0)), pl.BlockSpec((1, LANE), lambda i: (0, 0))],
        out_specs=[row] * 5,
        out_shape=[jax.ShapeDtypeStruct((T, LANE), F32)] * 5,
        compiler_params=_cparams(("parallel",)),
        name="rope_tables",
    )(pos, freqs)


def _inproj_kernel(x_ref, g_ref, w_ref, o_ref, h_ref):
    @pl.when(pl.program_id(1) == 0)
    def _():
        x = x_ref[...]
        ms = jnp.mean(x * x, axis=-1, keepdims=True)
        h_ref[...] = (x * lax.rsqrt(ms + EPS) * g_ref[...]).astype(BF16)

    o_ref[...] = _dot(h_ref[...], w_ref[...])


def _inproj(x, g, w, layer, tm=1024, tn=5 * MXU_DIM):
    T = x.shape[0]
    tm = min(tm, T)
    return pl.pallas_call(
        _inproj_kernel,
        grid=(T // tm, N_PACK // tn),
        in_specs=[
            pl.BlockSpec((tm, D_MODEL), lambda i, j: (i, 0)),
            pl.BlockSpec((1, D_MODEL), lambda i, j: (0, 0)),
            pl.BlockSpec((None, D_MODEL, tn), lambda i, j: (layer, 0, j)),
        ],
        out_specs=pl.BlockSpec((tm, tn), lambda i, j: (i, j)),
        out_shape=jax.ShapeDtypeStruct((T, N_PACK), F32),
        scratch_shapes=[pltpu.VMEM((tm, D_MODEL), BF16)],
        compiler_params=_cparams(("parallel", "arbitrary")),
        name="inproj",
    )(x, g, w)


def _outproj_kernel(x_ref, ya_ref, yb_ref, yc_ref, yd_ref, w_ref, o_ref):
    acc = x_ref[...]
    for n, y_ref in enumerate((ya_ref, yb_ref, yc_ref, yd_ref)):
        acc = acc + _dot(y_ref[...], w_ref[n * 512:(n + 1) * 512, :])
    o_ref[...] = acc


def _outproj(x, ya, yb, yc, yd, w, layer, tm=512):
    T = x.shape[0]
    tm = min(tm, T)
    yspec = pl.BlockSpec((tm, 512), lambda i: (i, 0))
    return pl.pallas_call(
        _outproj_kernel,
        grid=(T // tm,),
        in_specs=[pl.BlockSpec((tm, D_MODEL), lambda i: (i, 0)), yspec, yspec, yspec, yspec,
                  pl.BlockSpec((None, D_MIX, D_MODEL), lambda i: (layer, 0, 0))],
        out_specs=pl.BlockSpec((tm, D_MODEL), lambda i: (i, 0)),
        out_shape=jax.ShapeDtypeStruct((T, D_MODEL), F32),
        compiler_params=_cparams(("parallel",)),
        name="outproj",
    )(x, ya, yb, yc, yd, w)


def _gla_kernel(qk_ref, v_ref, gate_ref, sm_ref, w2_ref, b2_ref, ng_ref, o_ref, st_ref):
    @pl.when(pl.program_id(1) == 0)
    def _():
        st_ref[...] = jnp.zeros_like(st_ref)

    R, RB, C = GLA_ROWS, GLA_BLOCK, GLA_CHUNK
    xg = _dot(sm_ref[...].astype(BF16), w2_ref[...]) + b2_ref[...]
    logd = (jnp.minimum(xg, 0.0) - jnp.log(1.0 + jnp.exp(-jnp.abs(xg)))) * (1.0 / GLA_GATE_TAU)
    ri = lax.broadcasted_iota(jnp.int32, (RB, RB), 0)
    ci = lax.broadcasted_iota(jnp.int32, (RB, RB), 1)
    back = ri - ci
    in_chunk = (back >= 0) & (back <= (ri & (C - 1)))
    cum = jnp.where(in_chunk, 1.0, 0.0).astype(BF16)
    ng = ng_ref[...]
    n_chunks = RB // C
    hk = GLA_H * GLA_DK
    wide = n_chunks * GLA_DK
    place = ((lax.broadcasted_iota(jnp.int32, (RB, wide), 0) >> (C.bit_length() - 1))
             == (lax.broadcasted_iota(jnp.int32, (RB, wide), 1) >> (GLA_DK.bit_length() - 1)))

    def block_diag(x):
        pair = jnp.concatenate([x, x], axis=1)
        return jnp.where(place, jnp.concatenate([pair] * (n_chunks // 2), axis=1), jnp.zeros((), x.dtype))

    for sb in range(R // RB):
        rows = slice(sb * RB, (sb + 1) * RB)
        bc = _split_dot(cum, logd[rows]).reshape(n_chunks, C, hk)
        b_mid = bc[:, C // 2 - 1:C // 2, :]
        b_last = bc[:, C - 1:C, :]
        q = (qk_ref[rows, :hk] * (GLA_DK ** -0.5)).reshape(n_chunks, C, hk)
        k = qk_ref[rows, hk:].reshape(n_chunks, C, hk)
        qd = (q * jnp.exp(bc - b_mid)).astype(BF16).reshape(RB, hk)
        kd = (k * jnp.exp(b_mid - bc)).astype(BF16).reshape(RB, hk)
        qe = (q * jnp.exp(bc)).astype(BF16).reshape(RB, hk)
        kl = (k * jnp.exp(b_last - bc)).astype(BF16).reshape(RB, hk)
        e_last = jnp.exp(b_last)
        for h in range(GLA_H):
            kcols = slice(h * GLA_DK, (h + 1) * GLA_DK)
            vcols = slice(h * GLA_DV, (h + 1) * GLA_DV)
            v = v_ref[rows, vcols].astype(BF16)
            attn = jnp.where(in_chunk, _dot_nt(qd[:, kcols], kd[:, kcols]), 0.0).astype(BF16)
            u_all = _dot_tn(v, block_diag(kl[:, kcols]))
            st = st_ref[h]
            entering = []
            for c in range(n_chunks):
                entering.append(st)
                st = st * e_last[c][:, kcols] + u_all[:, c * GLA_DK:(c + 1) * GLA_DK]
            st_ref[h] = st
            s_all = jnp.concatenate(entering, axis=1).astype(BF16)
            o = _dot(attn, v) + _dot_nt(block_diag(qe[:, kcols]), s_all)
            ms = jnp.mean(o * o, axis=-1, keepdims=True)
            y = o * lax.rsqrt(ms + EPS) * ng * _silu(gate_ref[rows, vcols])
            o_ref[rows, vcols] = y.astype(BF16)


def _gla(u, w2p, b2, ng, B, S):
    R = GLA_ROWS
    nb = S // R
    row = lambda c: (lambda b, i: (b * nb + i, c))
    const = lambda b, i: (0, 0)
    return pl.pallas_call(
        _gla_kernel,
        grid=(B, nb),
        in_specs=[
            pl.BlockSpec((R, 512), row(COL_AQK // 512)),
            pl.BlockSpec((R, 512), row(COL_AV // 512)),
            pl.BlockSpec((R, 512), row(COL_GA // 512)),
            pl.BlockSpec((R, LANE), row(COL_SMALL // LANE)),
            pl.BlockSpec((LANE, GLA_H * GLA_DK), const),
            pl.BlockSpec((1, GLA_H * GLA_DK), const),
            pl.BlockSpec((1, GLA_DV), const),
        ],
        out_specs=pl.BlockSpec((R, 512), row(0)),
        out_shape=jax.ShapeDtypeStruct((B * S, GLA_H * GLA_DV), BF16),
        scratch_shapes=[pltpu.VMEM((GLA_H, GLA_DV, GLA_DK), F32)],
        compiler_params=_cparams(("parallel", "arbitrary")),
        name="gla",
    )(u, u, u, u, w2p, b2, ng)


def _ssd_kernel(xbc_ref, z_ref, sm_ref, cw_ref, cb_ref, dtb_ref, a_ref, d_ref, ng_ref, o_ref,
                xpad_ref, st_ref, y_ref):
    R, L = SSD_ROWS, SSD_CHUNK
    HALO = 8

    @pl.when(pl.program_id(1) == 0)
    def _():
        st_ref[...] = jnp.zeros_like(st_ref)
        xpad_ref[...] = jnp.zeros((HALO, SSD_CONV_DIM), F32)

    x = xbc_ref[...]
    halo = xpad_ref[...]
    row = lax.broadcasted_iota(jnp.int32, (HALO, 1), 0)
    conv = cb_ref[...] + x * cw_ref[SSD_CONV - 1:SSD_CONV, :]
    for k in range(1, SSD_CONV):
        shifted = pltpu.roll(x, k, 0)
        head = jnp.where(row < k, pltpu.roll(halo, k, 0), shifted[0:HALO])
        conv = conv + jnp.concatenate([head, shifted[HALO:]], axis=0) * cw_ref[SSD_CONV - 1 - k:SSD_CONV - k, :]
    xpad_ref[...] = x[R - HALO:R]
    xact = _silu(conv)

    lane = lax.broadcasted_iota(jnp.int32, (1, LANE), 1)
    is_dt = (lane >= SM_DT) & (lane < SM_DT + SSD_H)
    pre = sm_ref[...] + dtb_ref[...]
    dt_all = jnp.maximum(pre, 0.0) + jnp.log(1.0 + jnp.exp(-jnp.abs(pre)))
    a_all = jnp.where(is_dt, dt_all * -jnp.exp(a_ref[...]), 0.0)
    ri = lax.broadcasted_iota(jnp.int32, (L, L), 0)
    ci = lax.broadcasted_iota(jnp.int32, (L, L), 1)
    lower = ci <= ri
    cum = jnp.where(lower, 1.0, 0.0).astype(BF16)

    heads_per_group = SSD_H // SSD_G
    for sb in range(R // L):
        rows = slice(sb * L, (sb + 1) * L)
        cs_all = _split_dot(cum, a_all[rows])
        cs_t = cs_all.T
        for g in range(SSD_G):
            bm = xact[rows, SSD_DINNER + g * SSD_N:SSD_DINNER + (g + 1) * SSD_N].astype(BF16)
            cm = xact[rows, SSD_DINNER + SSD_G * SSD_N + g * SSD_N:
                      SSD_DINNER + SSD_G * SSD_N + (g + 1) * SSD_N].astype(BF16)
            scores = _dot_nt(cm, bm)
            for hh in range(heads_per_group):
                h = g * heads_per_group + hh
                hcols = slice(h * SSD_HD, (h + 1) * SSD_HD)
                xs = xact[rows, hcols]
                dt = dt_all[rows, SM_DT + h:SM_DT + h + 1]
                cs_col = cs_all[:, SM_DT + h:SM_DT + h + 1]
                cs_row = cs_t[SM_DT + h:SM_DT + h + 1, :]
                cs_last = cs_col[L - 1:L, :]
                decay = jnp.where(lower, jnp.exp(cs_col - cs_row), 0.0)
                xdt = xs * dt
                y = _dot((scores * decay).astype(BF16), xdt.astype(BF16))
                prev = st_ref[h]
                y = y + _dot_nt(cm, prev.astype(BF16)) * jnp.exp(cs_col)
                st_ref[h] = prev * jnp.exp(cs_last) + _dot_tn(
                    (xdt * jnp.exp(cs_last - cs_col)).astype(BF16), bm)
                y_ref[rows, hcols] = y + xs * d_ref[:, hcols]

    yz = y_ref[...] * _silu(z_ref[...])
    gw = SSD_DINNER // SSD_G
    for g in range(SSD_G):
        yg = yz[:, g * gw:(g + 1) * gw]
        ms = jnp.mean(yg * yg, axis=-1, keepdims=True)
        o_ref[:, g * gw:(g + 1) * gw] = (yg * lax.rsqrt(ms + EPS) * ng_ref[:, g * gw:(g + 1) * gw]).astype(BF16)


def _ssd(u, cw, cb, dtb, a_pad, d_exp, ng, B, S):
    L = SSD_ROWS
    nb = S // L
    row = lambda c: (lambda b, i: (b * nb + i, c))
    const = lambda b, i: (0, 0)
    return pl.pallas_call(
        _ssd_kernel,
        grid=(B, nb),
        in_specs=[
            pl.BlockSpec((L, SSD_CONV_DIM), row(COL_XBC // SSD_CONV_DIM)),
            pl.BlockSpec((L, 512), row(COL_GB // 512)),
            pl.BlockSpec((L, LANE), row(COL_SMALL // LANE)),
            pl.BlockSpec((SSD_CONV, SSD_CONV_DIM), const),
            pl.BlockSpec((1, SSD_CONV_DIM), const),
            pl.BlockSpec((1, LANE), const),
            pl.BlockSpec((1, LANE), const),
            pl.BlockSpec((1, SSD_DINNER), const),
            pl.BlockSpec((1, SSD_DINNER), const),
        ],
        out_specs=pl.BlockSpec((L, SSD_DINNER), row(0)),
        out_shape=jax.ShapeDtypeStruct((B * S, SSD_DINNER), BF16),
        scratch_shapes=[pltpu.VMEM((8, SSD_CONV_DIM), F32),
                        pltpu.VMEM((SSD_H, SSD_HD, SSD_N), F32),
                        pltpu.VMEM((L, SSD_DINNER), F32)],
        compiler_params=_cparams(("parallel", "arbitrary")),
        name="ssd",
    )(u, u, u, cw, cb, dtb, a_pad, d_exp, ng)


def _rope64(x, cm, s1, s2):
    return x * cm + pltpu.roll(x, LANE - MLA_ROPE // 2, 1) * s1 + pltpu.roll(x, MLA_ROPE // 2, 1) * s2


def _mla_prep_kernel(c_ref, sm_ref, cm_ref, s1_ref, s2_ref, wq_ref, wkv_ref, qng_ref, kvng_ref,
                     qhg_ref, khg_ref, q_ref, k_ref, v_ref):
    blk = c_ref[...]
    cq = blk[:, :MLA_Q_LORA]
    ckv = blk[:, MLA_Q_LORA:]
    def lane_tile_sum(x):
        tiles = [x[:, c:c + LANE] for c in range(0, x.shape[1], LANE)]
        return jnp.sum(functools.reduce(lambda a, b: a + b, tiles), axis=-1, keepdims=True)

    cqn = cq * lax.rsqrt(lane_tile_sum(cq * cq) * (1.0 / MLA_Q_LORA) + EPS) * qng_ref[...]
    ckvn = ckv * lax.rsqrt(lane_tile_sum(ckv * ckv) * (1.0 / MLA_KV_LORA) + EPS) * kvng_ref[...]
    qf = _dot(cqn.astype(BF16), wq_ref[...])
    kvf = _dot(ckvn.astype(BF16), wkv_ref[...])
    cm, s1, s2 = cm_ref[...], s1_ref[...], s2_ref[...]
    lane = lax.broadcasted_iota(jnp.int32, (1, LANE), 1)
    kr = jnp.where(lane < MLA_ROPE, sm_ref[...], 0.0)
    kr_sq = kr * kr
    qhg, khg = qhg_ref[...], khg_ref[...]
    kr_rot = _rope64(kr * khg[:, LANE:], cm, s1, s2)
    for h in range(MLA_H):
        qh = qf[:, 256 * h:256 * (h + 1)]
        rq = lax.rsqrt(lane_tile_sum(qh * qh) * (1.0 / MLA_QK) + EPS)
        rq = rq * (MLA_QK ** -0.5 * LOG2E)
        q_ref[0, h, :, 0:LANE] = (qh[:, :LANE] * rq * qhg[:, :LANE]).astype(BF16)
        q_ref[0, h, :, LANE:2 * LANE] = _rope64(qh[:, LANE:] * rq * qhg[:, LANE:], cm, s1, s2).astype(BF16)
        kn = kvf[:, 256 * h:256 * h + LANE]
        rk = lax.rsqrt(jnp.sum(kn * kn + kr_sq, axis=-1, keepdims=True) * (1.0 / MLA_QK) + EPS)
        k_ref[0, h, :, 0:LANE] = (kn * rk * khg[:, :LANE]).astype(BF16)
        k_ref[0, h, :, LANE:2 * LANE] = (kr_rot * rk).astype(BF16)
        v_ref[0, h, :, 0:MLA_V] = kvf[:, 256 * h + LANE:256 * (h + 1)].astype(BF16)
        v_ref[0, h, :, MLA_V:2 * MLA_V] = jnp.ones((kvf.shape[0], MLA_V), BF16)


def _mla_prep(u, cm, s1, s2, wq, wkv, qng, kvng, qhg, khg, B, S, ts=512):
    ts = min(ts, S)
    nb = S // ts
    row = lambda c: (lambda b, i: (b * nb + i, c))
    const = lambda b, i: (0, 0)
    hspec = lambda w: pl.BlockSpec((1, MLA_H, ts, w), lambda b, i: (b, 0, i, 0))
    return pl.pallas_call(
        _mla_prep_kernel,
        grid=(B, nb),
        in_specs=[
            pl.BlockSpec((ts, 512), row(COL_CQKV // 512)),
            pl.BlockSpec((ts, LANE), row(COL_SMALL // LANE)),
            pl.BlockSpec((ts, LANE), row(0)), pl.BlockSpec((ts, LANE), row(0)), pl.BlockSpec((ts, LANE), row(0)),
            pl.BlockSpec((MLA_Q_LORA, MLA_H * 256), const),
            pl.BlockSpec((MLA_KV_LORA, MLA_H * 256), const),
            pl.BlockSpec((1, MLA_Q_LORA), const),
            pl.BlockSpec((1, MLA_KV_LORA), const),
            pl.BlockSpec((1, 256), const),
            pl.BlockSpec((1, 256), const),
        ],
        out_specs=[hspec(256), hspec(256), hspec(2 * MLA_V)],
        out_shape=[jax.ShapeDtypeStruct((B, MLA_H, S, 256), BF16),
                   jax.ShapeDtypeStruct((B, MLA_H, S, 256), BF16),
                   jax.ShapeDtypeStruct((B, MLA_H, S, 2 * MLA_V), BF16)],
        compiler_params=_cparams(("parallel", "parallel")),
        name="mla_prep",
    )(u, u, cm, s1, s2, wq, wkv, qng, kvng, qhg, khg)


def _flash_kernel(q_ref, k_ref, v_ref, g_ref, o_ref, m_ref, acc_ref, sa_ref, sb_ref, *, t):
    i = pl.program_id(2)
    m_ref[...] = jnp.full_like(m_ref, -jnp.inf)
    acc_ref[...] = jnp.zeros_like(acc_ref)
    q = q_ref[0, 0]

    def rows(j):
        return pl.ds(pl.multiple_of(j * t, t), t)

    def scores(j):
        return _dot_nt(q, k_ref[0, 0, rows(j), :])

    def update(s, j, qrows=slice(0, t)):
        nk = s.shape[1]
        m_old = m_ref[qrows]
        m_new = jnp.maximum(m_old, jnp.max(s, axis=-1, keepdims=True))
        alpha = jnp.exp2(m_old - m_new)
        p = jnp.exp2(s - jnp.concatenate([m_new] * (nk // LANE), axis=1))
        vrows = pl.ds(pl.multiple_of(j * t, t), nk)
        acc_ref[qrows] = (jnp.concatenate([alpha, alpha], axis=1) * acc_ref[qrows]
                          + _dot(p.astype(BF16), v_ref[0, 0, vrows, :]))
        m_ref[qrows] = m_new

    def diagonal(s_ref):
        h = t // 2
        top = lax.broadcasted_iota(jnp.int32, (h, h), 1) <= lax.broadcasted_iota(jnp.int32, (h, h), 0)
        update(jnp.where(top, s_ref[0:h, 0:h], -jnp.inf), i, slice(0, h))
        low = lax.broadcasted_iota(jnp.int32, (h, t), 1) <= lax.broadcasted_iota(jnp.int32, (h, t), 0) + h
        update(jnp.where(low, s_ref[h:t, :], -jnp.inf), i, slice(h, t))

    sa_ref[...] = scores(0)

    def pair(p, carry):
        j = 2 * p
        s = sa_ref[...]
        sb_ref[...] = scores(j + 1)
        update(s, j)
        s = sb_ref[...]
        sa_ref[...] = scores(j + 2)
        update(s, j + 1)
        return carry

    lax.fori_loop(0, i // 2, pair, 0)

    @pl.when(i % 2 == 1)
    def _():
        s = sa_ref[...]
        sb_ref[...] = scores(i)
        update(s, i - 1)
        diagonal(sb_ref)

    @pl.when(i % 2 == 0)
    def _():
        diagonal(sa_ref)

    o_ref[...] = (acc_ref[:, :MLA_V] / acc_ref[:, MLA_V:] * _silu(g_ref[...])).astype(BF16)


def _flash(q, k, v, u, B, S, t=1024):
    t = min(t, S)
    nq = S // t
    return pl.pallas_call(
        functools.partial(_flash_kernel, t=t),
        grid=(B, MLA_H, nq),
        in_specs=[
            pl.BlockSpec((1, 1, t, 256), lambda b, h, i: (b, h, i, 0)),
            pl.BlockSpec((1, 1, S, 256), lambda b, h, i: (b, h, 0, 0)),
            pl.BlockSpec((1, 1, S, 2 * MLA_V), lambda b, h, i: (b, h, 0, 0)),
            pl.BlockSpec((t, LANE), lambda b, h, i: (b * nq + i, COL_GC // LANE + h)),
        ],
        out_specs=pl.BlockSpec((t, LANE), lambda b, h, i: (b * nq + i, h)),
        out_shape=jax.ShapeDtypeStruct((B * S, MLA_H * MLA_V), BF16),
        scratch_shapes=[pltpu.VMEM((t, LANE), F32), pltpu.VMEM((t, 2 * MLA_V), F32),
                        pltpu.VMEM((t, t), F32), pltpu.VMEM((t, t), F32)],
        compiler_params=_cparams(("parallel", "parallel", "arbitrary")),
        name="mla_flash",
    )(q, k, v, u)


def _dil_kernel(q_ref, k_ref, v_ref, g_ref, cd_ref, sd_ref, qg_ref, kg_ref, o_ref,
                qs_ref, ks_ref, qd_ref, kd_ref, vd_ref, oc_ref, ec_ref, tmp_ref, *, S):
    Q = DIL_BLK
    cd, sd = cd_ref[...], sd_ref[...]

    def norm_rope(x, g):
        xn = x * lax.rsqrt(jnp.mean(x * x, axis=-1, keepdims=True) + EPS) * g
        return xn * cd + pltpu.roll(xn, DIL_HD // 2, 1) * sd

    qs_ref[...] = norm_rope(q_ref[...], qg_ref[...] * (DIL_HD ** -0.5 * LOG2E))
    ks_ref[...] = norm_rope(k_ref[...], kg_ref[...])

    ra = lax.broadcasted_iota(jnp.int32, (Q, 2 * Q), 0)
    cc = lax.broadcasted_iota(jnp.int32, (Q, 2 * Q), 1)
    band = (cc >= ra) & (cc <= ra + Q)
    bias_inner = jnp.where(band, 0.0, -jnp.inf)
    bias_first = jnp.where(band & (cc >= Q), 0.0, -jnp.inf)
    n_units = S // Q
    vd_vals = vd_ref.at[:, 0:DIL_HD]
    vd_ref[:, DIL_HD:] = jnp.ones((vd_ref.shape[0], DIL_HD), BF16)

    for c, (w, d) in enumerate(DIL_CONFIGS):
        assert w // d == Q
        n_sub = S // d
        pitch = n_sub + Q
        nblk = n_sub // Q
        piece = min(n_sub, 512)
        streams = ((qs_ref, qd_ref, n_sub, 0), (ks_ref, kd_ref, pitch, Q), (v_ref, vd_vals, pitch, Q))
        for r in range(d):
            kd_ref[r * pitch:r * pitch + Q, :] = jnp.zeros((Q, DIL_HD), BF16)
            vd_vals[r * pitch:r * pitch + Q, :] = jnp.zeros((Q, DIL_HD), BF16)
        if d <= DIL_DIRECT_STRIDE:
            for r in range(d):
                for c0 in range(0, n_sub, piece):
                    src = pl.ds(r + c0 * d, piece, stride=d)
                    for src_ref, dst_ref, cpitch, lead in streams:
                        dst = r * cpitch + lead + c0
                        dst_ref[dst:dst + piece, :] = src_ref[src, :].astype(BF16)
        else:
            outer, inner = DIL_DIRECT_STRIDE, d // DIL_DIRECT_STRIDE
            n_outer = S // outer
            assert d % outer == 0 and inner <= DIL_DIRECT_STRIDE
            for r0 in range(outer):
                for src_ref, dst_ref, cpitch, lead in streams:
                    step = min(n_outer, 512)
                    for c0 in range(0, n_outer, step):
                        tmp_ref[c0:c0 + step, :] = src_ref[pl.ds(r0 + c0 * outer, step, stride=outer), :]
                    for m in range(inner):
                        dst = (r0 + outer * m) * cpitch + lead
                        dst_ref[dst:dst + n_sub, :] = tmp_ref[pl.ds(m, n_sub, stride=inner), :].astype(BF16)

        unroll = min(DIL_UNROLL, n_units)

        def group(n0, carry, c=c, n_sub=n_sub, pitch=pitch, nblk=nblk, unroll=unroll):
            for uu in range(unroll):
                n = n0 * unroll + uu
                r = n // nblk
                i = n - r * nblk
                qrows = pl.ds(pl.multiple_of(r * n_sub + i * Q, Q), Q)
                krows = pl.ds(pl.multiple_of(r * pitch + i * Q, Q), 2 * Q)
                s = _dot_nt(qd_ref[qrows, :], kd_ref[krows, :]) + jnp.where(i > 0, bias_inner, bias_first)
                m = jnp.max(s, axis=-1, keepdims=True)
                p = jnp.exp2(s - m)
                ol = _dot(p.astype(BF16), vd_ref[krows, :])
                l = ol[:, DIL_HD:]
                nat = pl.ds(i * (d * Q) + r, Q, stride=d) if d > 1 else pl.ds(pl.multiple_of(i * Q, Q), Q)
                oc_ref[c, nat, :] = ol[:, :DIL_HD] * (1.0 / l)
                ec_ref[c, nat, :] = m + jnp.log2(l)
            return carry

        lax.fori_loop(0, n_units // unroll, group, 0)

    rows_per_step = min(S, 256)

    def merge(n, carry):
        rows = pl.ds(pl.multiple_of(n * rows_per_step, rows_per_step), rows_per_step)
        es = [ec_ref[c, rows, :] for c in range(len(DIL_CONFIGS))]
        e_max = functools.reduce(jnp.maximum, es)
        ws = [jnp.exp2(e - e_max) for e in es]
        num = sum(wt * oc_ref[c, rows, :] for c, wt in enumerate(ws))
        o_ref[rows, :] = (num / sum(ws) * _silu(g_ref[rows, :])).astype(BF16)
        return carry

    lax.fori_loop(0, S // rows_per_step, merge, 0)


def _dil(u, cd, sd, qg, kg, B, S):
    col = lambda c: (lambda b, h: (b, c // LANE + h))
    const = lambda b, h: (0, 0)
    blk = lambda im: pl.BlockSpec((S, LANE), im)
    kv_rows = max(S + d * DIL_BLK for _, d in DIL_CONFIGS)
    table = pl.BlockSpec((S, LANE), lambda b, h: (b, 0), pipeline_mode=pl.Buffered(1))
    return pl.pallas_call(
        functools.partial(_dil_kernel, S=S),
        grid=(B, DIL_H),
        in_specs=[blk(col(COL_DQ)), blk(col(COL_DK)), blk(col(COL_DV)), blk(col(COL_GD)),
                  table, table,
                  pl.BlockSpec((1, DIL_HD), const), pl.BlockSpec((1, DIL_HD), const)],
        out_specs=blk(lambda b, h: (b, h)),
        out_shape=jax.ShapeDtypeStruct((B * S, DIL_H * DIL_HD), BF16),
        scratch_shapes=[pltpu.VMEM((S, DIL_HD), F32), pltpu.VMEM((S, DIL_HD), F32),
                        pltpu.VMEM((S, DIL_HD), BF16),
                        pltpu.VMEM((kv_rows, DIL_HD), BF16), pltpu.VMEM((kv_rows, 2 * DIL_HD), BF16),
                        pltpu.VMEM((len(DIL_CONFIGS), S, DIL_HD), F32),
                        pltpu.VMEM((len(DIL_CONFIGS), S, LANE), F32),
                        pltpu.VMEM((S // DIL_DIRECT_STRIDE, DIL_HD), F32)],
        compiler_params=_cparams(("parallel", "parallel")),
        name="dilated",
    )(u, u, u, u, cd, sd, qg, kg)


SRC_ALR, SRC_XBC, SRC_DT, SRC_CQKV, SRC_KR, SRC_DQKV = 3072, 3088, 4112, 4120, 4632, 4696
N_IN = 6232
PACK_RUNS = ((0, COL_XBC, 0), (SRC_XBC, SSD_CONV_DIM, COL_XBC), (SRC_CQKV, 512, COL_CQKV),
             (SRC_DQKV, 3 * 512, COL_DQ))
PACK_PIECE = 512


def _pack_kernel(wt_ref, o_ref):
    tc = wt_ref.shape[2]
    for src, width, dst in PACK_RUNS:
        for c0 in range(0, width, PACK_PIECE):
            o_ref[0, :, dst + c0:dst + c0 + PACK_PIECE] = (
                wt_ref[0, src + c0:src + c0 + PACK_PIECE, :].T.astype(BF16))
    small = jnp.concatenate(
        [wt_ref[0, SRC_KR:SRC_KR + MLA_ROPE, :], wt_ref[0, SRC_ALR:SRC_ALR + GLA_GATE_RANK, :],
         wt_ref[0, SRC_DT:SRC_DT + SSD_H, :], jnp.zeros((LANE - SM_DT - SSD_H, tc), F32)], axis=0)
    o_ref[0, :, COL_SMALL:COL_END] = small.T.astype(BF16)
    o_ref[0, :, COL_END:N_PACK] = jnp.zeros((tc, N_PACK - COL_END), BF16)


def _pack_w_in(w, tc=256):
    depth, kdim, n_in = w.shape
    assert n_in == N_IN and kdim % tc == 0 and all(width % PACK_PIECE == 0 for _, width, _ in PACK_RUNS)
    return pl.pallas_call(
        _pack_kernel,
        grid=(depth, kdim // tc),
        in_specs=[pl.BlockSpec((1, N_IN, tc), lambda l, i: (l, 0, i))],
        out_specs=pl.BlockSpec((1, tc, N_PACK), lambda l, i: (l, i, 0)),
        out_shape=jax.ShapeDtypeStruct((depth, kdim, N_PACK), BF16),
        compiler_params=_cparams(("parallel", "parallel")),
        name="pack_w_in",
    )(jnp.swapaxes(w, 1, 2))


def _row(v, width=None, offset=0):
    v = v.astype(F32).reshape(1, -1)
    if width is None:
        return v
    return jnp.pad(v, ((0, 0), (offset, width - offset - v.shape[1])))


def _pad_heads(w, n_heads, real, padded):
    k = w.shape[0]
    w = w.reshape(k, n_heads, real)
    return jnp.pad(w, ((0, 0), (0, 0), (0, padded - real))).reshape(k, n_heads * padded)


def kernel(x, positions, ln_g, w_in, w_out, gla_gate_w2, gla_gate_b, gla_norm_g, ssd_conv_w, ssd_conv_b,
           ssd_dt_bias, ssd_A_log, ssd_D, ssd_norm_g, mla_q_norm_g, mla_kv_norm_g, mla_w_uq, mla_w_ukv,
           mla_q_head_g, mla_k_head_g, dil_q_g, dil_k_g):
    B, S, D = x.shape
    depth = w_in.shape[0]
    assert D == D_MODEL and S % SSD_ROWS == 0 and S % GLA_ROWS == 0
    assert all(S % (d * DIL_BLK) == 0 for _, d in DIL_CONFIGS)
    T = B * S
    cm, s1, s2, cd, sd = _rope_tables(positions)
    w_in_packed = _pack_w_in(w_in)
    w_out_bf16 = w_out.astype(BF16)
    xf = x.reshape(T, D)
    for l in range(depth):
        u = _inproj(xf, _row(ln_g[l]), w_in_packed, l)
        w2p = jnp.pad(gla_gate_w2[l], ((SM_LR, LANE - SM_LR - GLA_GATE_RANK), (0, 0))).astype(BF16)
        ya = _gla(u, w2p, _row(gla_gate_b[l]), _row(gla_norm_g[l]), B, S)
        yb = _ssd(u, ssd_conv_w[l].astype(F32), _row(ssd_conv_b[l]),
                  _row(ssd_dt_bias[l], LANE, SM_DT), _row(ssd_A_log[l], LANE, SM_DT),
                  _row(jnp.repeat(ssd_D[l], SSD_HD)), _row(ssd_norm_g[l]), B, S)
        q, k, v = _mla_prep(u, cm, s1, s2,
                            _pad_heads(mla_w_uq[l], MLA_H, MLA_QK, 256).astype(BF16),
                            mla_w_ukv[l].astype(BF16), _row(mla_q_norm_g[l]), _row(mla_kv_norm_g[l]),
                            _row(mla_q_head_g[l], 256), _row(mla_k_head_g[l], 256), B, S)
        yc = _flash(q, k, v, u, B, S)
        yd = _dil(u, cd, sd, _row(dil_q_g[l]), _row(dil_k_g[l]), B, S)
        xf = _outproj(xf, ya, yb, yc, yd, w_out_bf16, l)
    return xf.reshape(B, S, D)
```

```python
import functools
import math

import jax
import jax.numpy as jnp
import numpy as np
from jax import lax
from jax.experimental import pallas as pl
from jax.experimental.pallas import tpu as pltpu

F32 = jnp.float32
BF16 = jnp.bfloat16

D_MODEL = 2048
GLA_H, GLA_DK, GLA_DV = 4, 64, 128
GLA_GATE_RANK = 16
GLA_GATE_TAU = 16.0
SSD_DINNER, SSD_HD, SSD_G, SSD_N, SSD_CONV = 512, 64, 2, 128, 4
SSD_H = SSD_DINNER // SSD_HD
SSD_CONV_DIM = SSD_DINNER + 2 * SSD_G * SSD_N
MLA_H, MLA_NOPE, MLA_ROPE, MLA_V = 4, 128, 64, 128
MLA_QK = MLA_NOPE + MLA_ROPE
MLA_Q_LORA, MLA_KV_LORA = 384, 128
DIL_H, DIL_HD = 4, 128
DIL_CONFIGS = ((128, 1), (512, 4), (2048, 16))
ROPE_THETA = 10000.0
EPS = 1e-6
LOG2E = math.log2(math.e)
D_MIX = GLA_H * GLA_DV + SSD_DINNER + MLA_H * MLA_V + DIL_H * DIL_HD

LANE = 128
SUBLANE = 8
MXU_DIM = 256
MIB = 1024 * 1024
MIX_W = 512
MLA_PITCH = 2 * LANE

COL_GA, COL_GB, COL_GC, COL_GD = 0, 512, 1024, 1536
COL_AQK = 2048
COL_AV = 2560
COL_XBC = 3072
COL_CQKV = 4096
COL_DQ, COL_DK, COL_DV = 4608, 5120, 5632
COL_SMALL = 6144
COL_END = COL_SMALL + LANE
INPROJ_TN = 5 * MXU_DIM
N_PACK = -(-COL_END // INPROJ_TN) * INPROJ_TN
SM_KR, SM_LR, SM_DT = 0, 64, 80

GLA_CHUNK = 32
GLA_ROWS = 1024
GLA_BLOCK = 512
SSD_CHUNK = 256
SSD_ROWS = 512
DIL_BLK = 128
DIL_UNROLL = 16
DIL_DIRECT_STRIDE = 4
DIL_PIECE = 512
DIL_MERGE_ROWS = 256
INPROJ_TM = 1024
OUTPROJ_TM = 512
PREP_ROWS = 512
FLASH_TILE = 1024
PACK_COLS = 256
PACK_PIECE = 512


def _cparams(sem, vmem_mib):
    return pltpu.CompilerParams(dimension_semantics=sem, vmem_limit_bytes=vmem_mib * MIB)


def _silu(x):
    return x * (1.0 / (1.0 + jnp.exp(-x)))


def _dot(a, b):
    return jnp.dot(a, b, preferred_element_type=F32)


def _dot_nt(a, b):
    return lax.dot_general(a, b, (((1,), (1,)), ((), ())), preferred_element_type=F32)


def _dot_tn(a, b):
    return lax.dot_general(a, b, (((0,), (0,)), ((), ())), preferred_element_type=F32)


def _split_dot(tri, x):
    hi = x.astype(BF16)
    lo = (x - hi.astype(F32)).astype(BF16)
    return _dot(tri, hi) + _dot(tri, lo)


def _rope_tables_kernel(pos_ref, f_ref, cm_ref, s1_ref, s2_ref, cd_ref, sd_ref):
    half = LANE // 2
    ang = pos_ref[...] * f_ref[...]
    cos, sin = jnp.cos(ang), jnp.sin(ang)
    lane = lax.broadcasted_iota(jnp.int32, (1, LANE), 1)
    low = lane < half
    cm_ref[...] = jnp.where(low, cos, 0.0)
    s1_ref[...] = jnp.where(lane < MLA_ROPE // 2, -sin, 0.0)
    s2_ref[...] = jnp.where((lane >= MLA_ROPE // 2) & low, sin, 0.0)
    cd_ref[...] = jnp.where(low, pltpu.roll(cos, half, 1), cos)
    sd_ref[...] = jnp.where(low, -pltpu.roll(sin, half, 1), sin)


def _rope_tables(positions):
    assert MLA_ROPE == LANE // 2 and DIL_HD == LANE
    T = positions.size
    ts = min(T, 2048)
    pos = positions.reshape(T, 1).astype(F32)
    lane = np.arange(LANE // 2)
    fm = np.exp(-math.log(ROPE_THETA) * (lane % (MLA_ROPE // 2)) * (2.0 / MLA_ROPE))
    fd = np.exp(-math.log(ROPE_THETA) * lane * (2.0 / DIL_HD))
    freqs = jnp.asarray(np.concatenate([fm, fd]), F32).reshape(1, LANE)
    row = pl.BlockSpec((ts, LANE), lambda i: (i, 0))
    return pl.pallas_call(
        _rope_tables_kernel,
        grid=(T // ts,),
        in_specs=[pl.BlockSpec((ts, 1), lambda i: (i, 0)), pl.BlockSpec((1, LANE), lambda i: (0, 0))],
        out_specs=[row] * 5,
        out_shape=[jax.ShapeDtypeStruct((T, LANE), F32)] * 5,
        compiler_params=_cparams(("parallel",), 24),
        name="rope_tables",
    )(pos, freqs)


def _inproj_kernel(x_ref, g_ref, w_ref, o_ref, h_ref):
    @pl.when(pl.program_id(1) == 0)
    def _():
        x = x_ref[...]
        ms = jnp.mean(x * x, axis=-1, keepdims=True)
        h_ref[...] = (x * lax.rsqrt(ms + EPS) * g_ref[...]).astype(BF16)

    o_ref[...] = _dot(h_ref[...], w_ref[...])


def _inproj(x, g, w, layer):
    T = x.shape[0]
    tm, tn = min(INPROJ_TM, T), INPROJ_TN
    return pl.pallas_call(
        _inproj_kernel,
        grid=(T // tm, N_PACK // tn),
        in_specs=[
            pl.BlockSpec((tm, D_MODEL), lambda i, j: (i, 0)),
            pl.BlockSpec((1, D_MODEL), lambda i, j: (0, 0)),
            pl.BlockSpec((None, D_MODEL, tn), lambda i, j: (layer, 0, j)),
        ],
        out_specs=pl.BlockSpec((tm, tn), lambda i, j: (i, j)),
        out_shape=jax.ShapeDtypeStruct((T, N_PACK), F32),
        scratch_shapes=[pltpu.VMEM((tm, D_MODEL), BF16)],
        compiler_params=_cparams(("parallel", "arbitrary"), 48),
        name="inproj",
    )(x, g, w)


def _outproj_kernel(x_ref, ya_ref, yb_ref, yc_ref, yd_ref, w_ref, o_ref):
    acc = x_ref[...]
    for n, y_ref in enumerate((ya_ref, yb_ref, yc_ref, yd_ref)):
        acc = acc + _dot(y_ref[...], w_ref[n * MIX_W:(n + 1) * MIX_W, :])
    o_ref[...] = acc


def _outproj(x, ya, yb, yc, yd, w, layer):
    T = x.shape[0]
    tm = min(OUTPROJ_TM, T)
    yspec = pl.BlockSpec((tm, MIX_W), lambda i: (i, 0))
    return pl.pallas_call(
        _outproj_kernel,
        grid=(T // tm,),
        in_specs=[pl.BlockSpec((tm, D_MODEL), lambda i: (i, 0)), yspec, yspec, yspec, yspec,
                  pl.BlockSpec((None, D_MIX, D_MODEL), lambda i: (layer, 0, 0))],
        out_specs=pl.BlockSpec((tm, D_MODEL), lambda i: (i, 0)),
        out_shape=jax.ShapeDtypeStruct((T, D_MODEL), F32),
        compiler_params=_cparams(("parallel",), 40),
        name="outproj",
    )(x, ya, yb, yc, yd, w)


def _gla_kernel(qk_ref, v_ref, gate_ref, sm_ref, w2_ref, b2_ref, ng_ref, o_ref, st_ref):
    @pl.when(pl.program_id(1) == 0)
    def _():
        st_ref[...] = jnp.zeros_like(st_ref)

    R, RB, C = GLA_ROWS, GLA_BLOCK, GLA_CHUNK
    xg = _dot(sm_ref[...].astype(BF16), w2_ref[...]) + b2_ref[...]
    logd = (jnp.minimum(xg, 0.0) - jnp.log(1.0 + jnp.exp(-jnp.abs(xg)))) * (1.0 / GLA_GATE_TAU)
    ri = lax.broadcasted_iota(jnp.int32, (RB, RB), 0)
    ci = lax.broadcasted_iota(jnp.int32, (RB, RB), 1)
    back = ri - ci
    in_chunk = (back >= 0) & (back <= (ri & (C - 1)))
    cum = jnp.where(in_chunk, 1.0, 0.0).astype(BF16)
    ng = ng_ref[...]
    n_chunks = RB // C
    hk = GLA_H * GLA_DK
    wide = n_chunks * GLA_DK
    place = ((lax.broadcasted_iota(jnp.int32, (RB, wide), 0) >> (C.bit_length() - 1))
             == (lax.broadcasted_iota(jnp.int32, (RB, wide), 1) >> (GLA_DK.bit_length() - 1)))

    def block_diag(x):
        pair = jnp.concatenate([x, x], axis=1)
        return jnp.where(place, jnp.concatenate([pair] * (n_chunks // 2), axis=1), jnp.zeros((), x.dtype))

    for sb in range(R // RB):
        rows = slice(sb * RB, (sb + 1) * RB)
        bc = _split_dot(cum, logd[rows]).reshape(n_chunks, C, hk)
        b_mid = bc[:, C // 2 - 1:C // 2, :]
        b_last = bc[:, C - 1:C, :]
        q = (qk_ref[rows, :hk] * (GLA_DK ** -0.5)).reshape(n_chunks, C, hk)
        k = qk_ref[rows, hk:].reshape(n_chunks, C, hk)
        qd = (q * jnp.exp(bc - b_mid)).astype(BF16).reshape(RB, hk)
        kd = (k * jnp.exp(b_mid - bc)).astype(BF16).reshape(RB, hk)
        qe = (q * jnp.exp(bc)).astype(BF16).reshape(RB, hk)
        kl = (k * jnp.exp(b_last - bc)).astype(BF16).reshape(RB, hk)
        e_last = jnp.exp(b_last)
        for h in range(GLA_H):
            kcols = slice(h * GLA_DK, (h + 1) * GLA_DK)
            vcols = slice(h * GLA_DV, (h + 1) * GLA_DV)
            v = v_ref[rows, vcols].astype(BF16)
            attn = jnp.where(in_chunk, _dot_nt(qd[:, kcols], kd[:, kcols]), 0.0).astype(BF16)
            u_all = _dot_tn(v, block_diag(kl[:, kcols]))
            st = st_ref[h]
            entering = []
            for c in range(n_chunks):
                entering.append(st)
                st = st * e_last[c][:, kcols] + u_all[:, c * GLA_DK:(c + 1) * GLA_DK]
            st_ref[h] = st
            s_all = jnp.concatenate(entering, axis=1).astype(BF16)
            o = _dot(attn, v) + _dot_nt(block_diag(qe[:, kcols]), s_all)
            ms = jnp.mean(o * o, axis=-1, keepdims=True)
            y = o * lax.rsqrt(ms + EPS) * ng * _silu(gate_ref[rows, vcols])
            o_ref[rows, vcols] = y.astype(BF16)


def _gla(u, w2p, b2, ng, B, S):
    R = GLA_ROWS
    nb = S // R
    row = lambda c: (lambda b, i: (b * nb + i, c))
    const = lambda b, i: (0, 0)
    return pl.pallas_call(
        _gla_kernel,
        grid=(B, nb),
        in_specs=[
            pl.BlockSpec((R, MIX_W), row(COL_AQK // MIX_W)),
            pl.BlockSpec((R, MIX_W), row(COL_AV // MIX_W)),
            pl.BlockSpec((R, MIX_W), row(COL_GA // MIX_W)),
            pl.BlockSpec((R, LANE), row(COL_SMALL // LANE)),
            pl.BlockSpec((LANE, GLA_H * GLA_DK), const),
            pl.BlockSpec((1, GLA_H * GLA_DK), const),
            pl.BlockSpec((1, GLA_DV), const),
        ],
        out_specs=pl.BlockSpec((R, MIX_W), row(0)),
        out_shape=jax.ShapeDtypeStruct((B * S, GLA_H * GLA_DV), BF16),
        scratch_shapes=[pltpu.VMEM((GLA_H, GLA_DV, GLA_DK), F32)],
        compiler_params=_cparams(("parallel", "arbitrary"), 32),
        name="gla",
    )(u, u, u, u, w2p, b2, ng)


def _ssd_kernel(xbc_ref, z_ref, sm_ref, cw_ref, cb_ref, dtb_ref, a_ref, d_ref, ng_ref, o_ref,
                xpad_ref, st_ref, y_ref):
    R, L = SSD_ROWS, SSD_CHUNK
    HALO = SUBLANE

    @pl.when(pl.program_id(1) == 0)
    def _():
        st_ref[...] = jnp.zeros_like(st_ref)
        xpad_ref[...] = jnp.zeros((HALO, SSD_CONV_DIM), F32)

    x = xbc_ref[...]
    halo = xpad_ref[...]
    row = lax.broadcasted_iota(jnp.int32, (HALO, 1), 0)
    conv = cb_ref[...] + x * cw_ref[SSD_CONV - 1:SSD_CONV, :]
    for k in range(1, SSD_CONV):
        shifted = pltpu.roll(x, k, 0)
        head = jnp.where(row < k, pltpu.roll(halo, k, 0), shifted[0:HALO])
        conv = conv + jnp.concatenate([head, shifted[HALO:]], axis=0) * cw_ref[SSD_CONV - 1 - k:SSD_CONV - k, :]
    xpad_ref[...] = x[R - HALO:R]
    xact = _silu(conv)

    lane = lax.broadcasted_iota(jnp.int32, (1, LANE), 1)
    is_dt = (lane >= SM_DT) & (lane < SM_DT + SSD_H)
    pre = sm_ref[...] + dtb_ref[...]
    dt_all = jnp.maximum(pre, 0.0) + jnp.log(1.0 + jnp.exp(-jnp.abs(pre)))
    a_all = jnp.where(is_dt, dt_all * -jnp.exp(a_ref[...]), 0.0)
    ri = lax.broadcasted_iota(jnp.int32, (L, L), 0)
    ci = lax.broadcasted_iota(jnp.int32, (L, L), 1)
    lower = ci <= ri
    cum = jnp.where(lower, 1.0, 0.0).astype(BF16)

    heads_per_group = SSD_H // SSD_G
    for sb in range(R // L):
        rows = slice(sb * L, (sb + 1) * L)
        cs_all = _split_dot(cum, a_all[rows])
        cs_t = cs_all.T
        for g in range(SSD_G):
            bm = xact[rows, SSD_DINNER + g * SSD_N:SSD_DINNER + (g + 1) * SSD_N].astype(BF16)
            cm = xact[rows, SSD_DINNER + SSD_G * SSD_N + g * SSD_N:
                      SSD_DINNER + SSD_G * SSD_N + (g + 1) * SSD_N].astype(BF16)
            scores = _dot_nt(cm, bm)
            for hh in range(heads_per_group):
                h = g * heads_per_group + hh
                hcols = slice(h * SSD_HD, (h + 1) * SSD_HD)
                xs = xact[rows, hcols]
                dt = dt_all[rows, SM_DT + h:SM_DT + h + 1]
                cs_col = cs_all[:, SM_DT + h:SM_DT + h + 1]
                cs_row = cs_t[SM_DT + h:SM_DT + h + 1, :]
                cs_last = cs_col[L - 1:L, :]
                decay = jnp.where(lower, jnp.exp(cs_col - cs_row), 0.0)
                xdt = xs * dt
                y = _dot((scores * decay).astype(BF16), xdt.astype(BF16))
                prev = st_ref[h]
                y = y + _dot_nt(cm, prev.astype(BF16)) * jnp.exp(cs_col)
                st_ref[h] = prev * jnp.exp(cs_last) + _dot_tn(
                    (xdt * jnp.exp(cs_last - cs_col)).astype(BF16), bm)
                y_ref[rows, hcols] = y + xs * d_ref[:, hcols]

    yz = y_ref[...] * _silu(z_ref[...])
    gw = SSD_DINNER // SSD_G
    for g in range(SSD_G):
        yg = yz[:, g * gw:(g + 1) * gw]
        ms = jnp.mean(yg * yg, axis=-1, keepdims=True)
        o_ref[:, g * gw:(g + 1) * gw] = (yg * lax.rsqrt(ms + EPS) * ng_ref[:, g * gw:(g + 1) * gw]).astype(BF16)


def _ssd(u, cw, cb, dtb, a_pad, d_exp, ng, B, S):
    L = SSD_ROWS
    nb = S // L
    row = lambda c: (lambda b, i: (b * nb + i, c))
    const = lambda b, i: (0, 0)
    return pl.pallas_call(
        _ssd_kernel,
        grid=(B, nb),
        in_specs=[
            pl.BlockSpec((L, SSD_CONV_DIM), row(COL_XBC // SSD_CONV_DIM)),
            pl.BlockSpec((L, MIX_W), row(COL_GB // MIX_W)),
            pl.BlockSpec((L, LANE), row(COL_SMALL // LANE)),
            pl.BlockSpec((SSD_CONV, SSD_CONV_DIM), const),
            pl.BlockSpec((1, SSD_CONV_DIM), const),
            pl.BlockSpec((1, LANE), const),
            pl.BlockSpec((1, LANE), const),
            pl.BlockSpec((1, SSD_DINNER), const),
            pl.BlockSpec((1, SSD_DINNER), const),
        ],
        out_specs=pl.BlockSpec((L, SSD_DINNER), row(0)),
        out_shape=jax.ShapeDtypeStruct((B * S, SSD_DINNER), BF16),
        scratch_shapes=[pltpu.VMEM((SUBLANE, SSD_CONV_DIM), F32),
                        pltpu.VMEM((SSD_H, SSD_HD, SSD_N), F32),
                        pltpu.VMEM((L, SSD_DINNER), F32)],
        compiler_params=_cparams(("parallel", "arbitrary"), 24),
        name="ssd",
    )(u, u, u, cw, cb, dtb, a_pad, d_exp, ng)


def _rope64(x, cm, s1, s2):
    return x * cm + pltpu.roll(x, LANE - MLA_ROPE // 2, 1) * s1 + pltpu.roll(x, MLA_ROPE // 2, 1) * s2


def _mla_prep_kernel(c_ref, sm_ref, cm_ref, s1_ref, s2_ref, wq_ref, wkv_ref, qng_ref, kvng_ref,
                     qhg_ref, khg_ref, q_ref, k_ref, v_ref):
    blk = c_ref[...]
    cq = blk[:, :MLA_Q_LORA]
    ckv = blk[:, MLA_Q_LORA:]
    def lane_tile_sum(x):
        tiles = [x[:, c:c + LANE] for c in range(0, x.shape[1], LANE)]
        return jnp.sum(functools.reduce(lambda a, b: a + b, tiles), axis=-1, keepdims=True)

    cqn = cq * lax.rsqrt(lane_tile_sum(cq * cq) * (1.0 / MLA_Q_LORA) + EPS) * qng_ref[...]
    ckvn = ckv * lax.rsqrt(lane_tile_sum(ckv * ckv) * (1.0 / MLA_KV_LORA) + EPS) * kvng_ref[...]
    qf = _dot(cqn.astype(BF16), wq_ref[...])
    kvf = _dot(ckvn.astype(BF16), wkv_ref[...])
    cm, s1, s2 = cm_ref[...], s1_ref[...], s2_ref[...]
    lane = lax.broadcasted_iota(jnp.int32, (1, LANE), 1)
    kr = jnp.where(lane < MLA_ROPE, sm_ref[...], 0.0)
    kr_sq = kr * kr
    qhg, khg = qhg_ref[...], khg_ref[...]
    kr_rot = _rope64(kr * khg[:, LANE:], cm, s1, s2)
    for h in range(MLA_H):
        qh = qf[:, MLA_PITCH * h:MLA_PITCH * (h + 1)]
        rq = lax.rsqrt(lane_tile_sum(qh * qh) * (1.0 / MLA_QK) + EPS)
        rq = rq * (MLA_QK ** -0.5 * LOG2E)
        q_ref[0, h, :, 0:LANE] = (qh[:, :LANE] * rq * qhg[:, :LANE]).astype(BF16)
        q_ref[0, h, :, LANE:2 * LANE] = _rope64(qh[:, LANE:] * rq * qhg[:, LANE:], cm, s1, s2).astype(BF16)
        kn = kvf[:, MLA_PITCH * h:MLA_PITCH * h + LANE]
        rk = lax.rsqrt(jnp.sum(kn * kn + kr_sq, axis=-1, keepdims=True) * (1.0 / MLA_QK) + EPS)
        k_ref[0, h, :, 0:LANE] = (kn * rk * khg[:, :LANE]).astype(BF16)
        k_ref[0, h, :, LANE:2 * LANE] = (kr_rot * rk).astype(BF16)
        v_ref[0, h, :, 0:MLA_V] = kvf[:, MLA_PITCH * h + LANE:MLA_PITCH * (h + 1)].astype(BF16)
        v_ref[0, h, :, MLA_V:2 * MLA_V] = jnp.ones((kvf.shape[0], MLA_V), BF16)


def _mla_prep(u, cm, s1, s2, wq, wkv, qng, kvng, qhg, khg, B, S):
    ts = min(PREP_ROWS, S)
    nb = S // ts
    row = lambda c: (lambda b, i: (b * nb + i, c))
    const = lambda b, i: (0, 0)
    hspec = lambda w: pl.BlockSpec((1, MLA_H, ts, w), lambda b, i: (b, 0, i, 0))
    return pl.pallas_call(
        _mla_prep_kernel,
        grid=(B, nb),
        in_specs=[
            pl.BlockSpec((ts, MIX_W), row(COL_CQKV // MIX_W)),
            pl.BlockSpec((ts, LANE), row(COL_SMALL // LANE)),
            pl.BlockSpec((ts, LANE), row(0)), pl.BlockSpec((ts, LANE), row(0)), pl.BlockSpec((ts, LANE), row(0)),
            pl.BlockSpec((MLA_Q_LORA, MLA_H * MLA_PITCH), const),
            pl.BlockSpec((MLA_KV_LORA, MLA_H * MLA_PITCH), const),
            pl.BlockSpec((1, MLA_Q_LORA), const),
            pl.BlockSpec((1, MLA_KV_LORA), const),
            pl.BlockSpec((1, MLA_PITCH), const),
            pl.BlockSpec((1, MLA_PITCH), const),
        ],
        out_specs=[hspec(MLA_PITCH), hspec(MLA_PITCH), hspec(2 * MLA_V)],
        out_shape=[jax.ShapeDtypeStruct((B, MLA_H, S, MLA_PITCH), BF16),
                   jax.ShapeDtypeStruct((B, MLA_H, S, MLA_PITCH), BF16),
                   jax.ShapeDtypeStruct((B, MLA_H, S, 2 * MLA_V), BF16)],
        compiler_params=_cparams(("parallel", "parallel"), 24),
        name="mla_prep",
    )(u, u, cm, s1, s2, wq, wkv, qng, kvng, qhg, khg)


def _flash_kernel(q_ref, k_ref, v_ref, g_ref, o_ref, m_ref, acc_ref, sa_ref, sb_ref, *, t):
    i = pl.program_id(2)
    m_ref[...] = jnp.full_like(m_ref, -jnp.inf)
    acc_ref[...] = jnp.zeros_like(acc_ref)
    q = q_ref[0, 0]

    def rows(j):
        return pl.ds(pl.multiple_of(j * t, t), t)

    def scores(j):
        return _dot_nt(q, k_ref[0, 0, rows(j), :])

    def update(s, j, qrows=slice(0, t)):
        nk = s.shape[1]
        m_old = m_ref[qrows]
        m_new = jnp.maximum(m_old, jnp.max(s, axis=-1, keepdims=True))
        alpha = jnp.exp2(m_old - m_new)
        p = jnp.exp2(s - jnp.concatenate([m_new] * (nk // LANE), axis=1))
        vrows = pl.ds(pl.multiple_of(j * t, t), nk)
        acc_ref[qrows] = (jnp.concatenate([alpha, alpha], axis=1) * acc_ref[qrows]
                          + _dot(p.astype(BF16), v_ref[0, 0, vrows, :]))
        m_ref[qrows] = m_new

    def diagonal(s_ref):
        h = t // 2
        top = lax.broadcasted_iota(jnp.int32, (h, h), 1) <= lax.broadcasted_iota(jnp.int32, (h, h), 0)
        update(jnp.where(top, s_ref[0:h, 0:h], -jnp.inf), i, slice(0, h))
        low = lax.broadcasted_iota(jnp.int32, (h, t), 1) <= lax.broadcasted_iota(jnp.int32, (h, t), 0) + h
        update(jnp.where(low, s_ref[h:t, :], -jnp.inf), i, slice(h, t))

    sa_ref[...] = scores(0)

    def pair(p, carry):
        j = 2 * p
        s = sa_ref[...]
        sb_ref[...] = scores(j + 1)
        update(s, j)
        s = sb_ref[...]
        sa_ref[...] = scores(j + 2)
        update(s, j + 1)
        return carry

    lax.fori_loop(0, i // 2, pair, 0)

    @pl.when(i % 2 == 1)
    def _():
        s = sa_ref[...]
        sb_ref[...] = scores(i)
        update(s, i - 1)
        diagonal(sb_ref)

    @pl.when(i % 2 == 0)
    def _():
        diagonal(sa_ref)

    o_ref[...] = (acc_ref[:, :MLA_V] / acc_ref[:, MLA_V:] * _silu(g_ref[...])).astype(BF16)


def _flash(q, k, v, u, B, S):
    t = min(FLASH_TILE, S)
    nq = S // t
    return pl.pallas_call(
        functools.partial(_flash_kernel, t=t),
        grid=(B, MLA_H, nq),
        in_specs=[
            pl.BlockSpec((1, 1, t, MLA_PITCH), lambda b, h, i: (b, h, i, 0)),
            pl.BlockSpec((1, 1, S, MLA_PITCH), lambda b, h, i: (b, h, 0, 0)),
            pl.BlockSpec((1, 1, S, 2 * MLA_V), lambda b, h, i: (b, h, 0, 0)),
            pl.BlockSpec((t, LANE), lambda b, h, i: (b * nq + i, COL_GC // LANE + h)),
        ],
        out_specs=pl.BlockSpec((t, LANE), lambda b, h, i: (b * nq + i, h)),
        out_shape=jax.ShapeDtypeStruct((B * S, MLA_H * MLA_V), BF16),
        scratch_shapes=[pltpu.VMEM((t, LANE), F32), pltpu.VMEM((t, 2 * MLA_V), F32),
                        pltpu.VMEM((t, t), F32), pltpu.VMEM((t, t), F32)],
        compiler_params=_cparams(("parallel", "parallel", "arbitrary"), 32),
        name="mla_flash",
    )(q, k, v, u)


def _dil_kernel(q_ref, k_ref, v_ref, g_ref, cd_ref, sd_ref, qg_ref, kg_ref, o_ref,
                qs_ref, ks_ref, qd_ref, kd_ref, vd_ref, oc_ref, ec_ref, tmp_ref, *, S):
    Q = DIL_BLK
    cd, sd = cd_ref[...], sd_ref[...]

    def norm_rope(x, g):
        xn = x * lax.rsqrt(jnp.mean(x * x, axis=-1, keepdims=True) + EPS) * g
        return xn * cd + pltpu.roll(xn, DIL_HD // 2, 1) * sd

    qs_ref[...] = norm_rope(q_ref[...], qg_ref[...] * (DIL_HD ** -0.5 * LOG2E))
    ks_ref[...] = norm_rope(k_ref[...], kg_ref[...])

    ra = lax.broadcasted_iota(jnp.int32, (Q, 2 * Q), 0)
    cc = lax.broadcasted_iota(jnp.int32, (Q, 2 * Q), 1)
    band = (cc >= ra) & (cc <= ra + Q)
    bias_inner = jnp.where(band, 0.0, -jnp.inf)
    bias_first = jnp.where(band & (cc >= Q), 0.0, -jnp.inf)
    n_units = S // Q
    vd_vals = vd_ref.at[:, 0:DIL_HD]
    vd_ref[:, DIL_HD:] = jnp.ones((vd_ref.shape[0], DIL_HD), BF16)

    for c, (w, d) in enumerate(DIL_CONFIGS):
        assert w // d == Q
        n_sub = S // d
        pitch = n_sub + Q
        nblk = n_sub // Q
        piece = min(n_sub, DIL_PIECE)
        streams = ((qs_ref, qd_ref, n_sub, 0), (ks_ref, kd_ref, pitch, Q), (v_ref, vd_vals, pitch, Q))
        for r in range(d):
            kd_ref[r * pitch:r * pitch + Q, :] = jnp.zeros((Q, DIL_HD), BF16)
            vd_vals[r * pitch:r * pitch + Q, :] = jnp.zeros((Q, DIL_HD), BF16)
        if d <= DIL_DIRECT_STRIDE:
            for r in range(d):
                for c0 in range(0, n_sub, piece):
                    src = pl.ds(r + c0 * d, piece, stride=d)
                    for src_ref, dst_ref, cpitch, lead in streams:
                        dst = r * cpitch + lead + c0
                        dst_ref[dst:dst + piece, :] = src_ref[src, :].astype(BF16)
        else:
            outer, inner = DIL_DIRECT_STRIDE, d // DIL_DIRECT_STRIDE
            n_outer = S // outer
            assert d % outer == 0 and inner <= DIL_DIRECT_STRIDE
            for r0 in range(outer):
                for src_ref, dst_ref, cpitch, lead in streams:
                    step = min(n_outer, DIL_PIECE)
                    for c0 in range(0, n_outer, step):
                        tmp_ref[c0:c0 + step, :] = src_ref[pl.ds(r0 + c0 * outer, step, stride=outer), :]
                    for m in range(inner):
                        dst = (r0 + outer * m) * cpitch + lead
                        dst_ref[dst:dst + n_sub, :] = tmp_ref[pl.ds(m, n_sub, stride=inner), :].astype(BF16)

        unroll = min(DIL_UNROLL, n_units)

        def group(n0, carry, c=c, n_sub=n_sub, pitch=pitch, nblk=nblk, unroll=unroll):
            for uu in range(unroll):
                n = n0 * unroll + uu
                r = n // nblk
                i = n - r * nblk
                qrows = pl.ds(pl.multiple_of(r * n_sub + i * Q, Q), Q)
                krows = pl.ds(pl.multiple_of(r * pitch + i * Q, Q), 2 * Q)
                s = _dot_nt(qd_ref[qrows, :], kd_ref[krows, :]) + jnp.where(i > 0, bias_inner, bias_first)
                m = jnp.max(s, axis=-1, keepdims=True)
                p = jnp.exp2(s - m)
                ol = _dot(p.astype(BF16), vd_ref[krows, :])
                l = ol[:, DIL_HD:]
                nat = pl.ds(i * (d * Q) + r, Q, stride=d) if d > 1 else pl.ds(pl.multiple_of(i * Q, Q), Q)
                oc_ref[c, nat, :] = ol[:, :DIL_HD] * (1.0 / l)
                ec_ref[c, nat, :] = m + jnp.log2(l)
            return carry

        lax.fori_loop(0, n_units // unroll, group, 0)

    rows_per_step = min(S, DIL_MERGE_ROWS)

    def merge(n, carry):
        rows = pl.ds(pl.multiple_of(n * rows_per_step, rows_per_step), rows_per_step)
        es = [ec_ref[c, rows, :] for c in range(len(DIL_CONFIGS))]
        e_max = functools.reduce(jnp.maximum, es)
        ws = [jnp.exp2(e - e_max) for e in es]
        num = sum(wt * oc_ref[c, rows, :] for c, wt in enumerate(ws))
        o_ref[rows, :] = (num / sum(ws) * _silu(g_ref[rows, :])).astype(BF16)
        return carry

    lax.fori_loop(0, S // rows_per_step, merge, 0)


def _dil(u, cd, sd, qg, kg, B, S):
    col = lambda c: (lambda b, h: (b, c // LANE + h))
    const = lambda b, h: (0, 0)
    blk = lambda im: pl.BlockSpec((S, LANE), im)
    kv_rows = max(S + d * DIL_BLK for _, d in DIL_CONFIGS)
    table = pl.BlockSpec((S, LANE), lambda b, h: (b, 0), pipeline_mode=pl.Buffered(1))
    return pl.pallas_call(
        functools.partial(_dil_kernel, S=S),
        grid=(B, DIL_H),
        in_specs=[blk(col(COL_DQ)), blk(col(COL_DK)), blk(col(COL_DV)), blk(col(COL_GD)),
                  table, table,
                  pl.BlockSpec((1, DIL_HD), const), pl.BlockSpec((1, DIL_HD), const)],
        out_specs=blk(lambda b, h: (b, h)),
        out_shape=jax.ShapeDtypeStruct((B * S, DIL_H * DIL_HD), BF16),
        scratch_shapes=[pltpu.VMEM((S, DIL_HD), F32), pltpu.VMEM((S, DIL_HD), F32),
                        pltpu.VMEM((S, DIL_HD), BF16),
                        pltpu.VMEM((kv_rows, DIL_HD), BF16), pltpu.VMEM((kv_rows, 2 * DIL_HD), BF16),
                        pltpu.VMEM((len(DIL_CONFIGS), S, DIL_HD), F32),
                        pltpu.VMEM((len(DIL_CONFIGS), S, LANE), F32),
                        pltpu.VMEM((S // DIL_DIRECT_STRIDE, DIL_HD), F32)],
        compiler_params=_cparams(("parallel", "parallel"), 52),
        name="dilated",
    )(u, u, u, u, cd, sd, qg, kg)


SRC_ALR, SRC_XBC, SRC_DT, SRC_CQKV, SRC_KR, SRC_DQKV = 3072, 3088, 4112, 4120, 4632, 4696
N_IN = 6232
PACK_RUNS = ((0, COL_XBC, 0), (SRC_XBC, SSD_CONV_DIM, COL_XBC), (SRC_CQKV, MLA_Q_LORA + MLA_KV_LORA, COL_CQKV),
             (SRC_DQKV, 3 * MIX_W, COL_DQ))


def _pack_kernel(wt_ref, o_ref):
    tc = wt_ref.shape[2]
    for src, width, dst in PACK_RUNS:
        for c0 in range(0, width, PACK_PIECE):
            o_ref[0, :, dst + c0:dst + c0 + PACK_PIECE] = (
                wt_ref[0, src + c0:src + c0 + PACK_PIECE, :].T.astype(BF16))
    small = jnp.concatenate(
        [wt_ref[0, SRC_KR:SRC_KR + MLA_ROPE, :], wt_ref[0, SRC_ALR:SRC_ALR + GLA_GATE_RANK, :],
         wt_ref[0, SRC_DT:SRC_DT + SSD_H, :], jnp.zeros((LANE - SM_DT - SSD_H, tc), F32)], axis=0)
    o_ref[0, :, COL_SMALL:COL_END] = small.T.astype(BF16)
    o_ref[0, :, COL_END:N_PACK] = jnp.zeros((tc, N_PACK - COL_END), BF16)


def _pack_w_in(w):
    depth, kdim, n_in = w.shape
    assert n_in == N_IN and kdim % PACK_COLS == 0 and all(width % PACK_PIECE == 0 for _, width, _ in PACK_RUNS)
    return pl.pallas_call(
        _pack_kernel,
        grid=(depth, kdim // PACK_COLS),
        in_specs=[pl.BlockSpec((1, N_IN, PACK_COLS), lambda l, i: (l, 0, i))],
        out_specs=pl.BlockSpec((1, PACK_COLS, N_PACK), lambda l, i: (l, i, 0)),
        out_shape=jax.ShapeDtypeStruct((depth, kdim, N_PACK), BF16),
        compiler_params=_cparams(("parallel", "parallel"), 28),
        name="pack_w_in",
    )(jnp.swapaxes(w, 1, 2))


def _row(v, width=None, offset=0):
    v = v.astype(F32).reshape(1, -1)
    if width is None:
        return v
    return jnp.pad(v, ((0, 0), (offset, width - offset - v.shape[1])))


def _pad_heads(w, n_heads, real, padded):
    k = w.shape[0]
    w = w.reshape(k, n_heads, real)
    return jnp.pad(w, ((0, 0), (0, 0), (0, padded - real))).reshape(k, n_heads * padded)


def kernel(x, positions, ln_g, w_in, w_out, gla_gate_w2, gla_gate_b, gla_norm_g, ssd_conv_w, ssd_conv_b,
           ssd_dt_bias, ssd_A_log, ssd_D, ssd_norm_g, mla_q_norm_g, mla_kv_norm_g, mla_w_uq, mla_w_ukv,
           mla_q_head_g, mla_k_head_g, dil_q_g, dil_k_g):
    B, S, D = x.shape
    depth = w_in.shape[0]
    assert D == D_MODEL and S % SSD_ROWS == 0 and S % GLA_ROWS == 0
    assert all(S % (d * DIL_BLK) == 0 for _, d in DIL_CONFIGS)
    T = B * S
    cm, s1, s2, cd, sd = _rope_tables(positions)
    w_in_packed = _pack_w_in(w_in)
    w_out_bf16 = w_out.astype(BF16)
    xf = x.reshape(T, D)
    for l in range(depth):
        u = _inproj(xf, _row(ln_g[l]), w_in_packed, l)
        w2p = jnp.pad(gla_gate_w2[l], ((SM_LR, LANE - SM_LR - GLA_GATE_RANK), (0, 0))).astype(BF16)
        ya = _gla(u, w2p, _row(gla_gate_b[l]), _row(gla_norm_g[l]), B, S)
        yb = _ssd(u, ssd_conv_w[l].astype(F32), _row(ssd_conv_b[l]),
                  _row(ssd_dt_bias[l], LANE, SM_DT), _row(ssd_A_log[l], LANE, SM_DT),
                  _row(jnp.repeat(ssd_D[l], SSD_HD)), _row(ssd_norm_g[l]), B, S)
        q, k, v = _mla_prep(u, cm, s1, s2,
                            _pad_heads(mla_w_uq[l], MLA_H, MLA_QK, MLA_PITCH).astype(BF16),
                            mla_w_ukv[l].astype(BF16), _row(mla_q_norm_g[l]), _row(mla_kv_norm_g[l]),
                            _row(mla_q_head_g[l], MLA_PITCH), _row(mla_k_head_g[l], MLA_PITCH), B, S)
        yc = _flash(q, k, v, u, B, S)
        yd = _dil(u, cd, sd, _row(dil_q_g[l]), _row(dil_k_g[l]), B, S)
        xf = _outproj(xf, ya, yb, yc, yd, w_out_bf16, l)
    return xf.reshape(B, S, D)
```

```python
import functools
import math

import jax
import jax.numpy as jnp
import numpy as np
from jax import lax
from jax.experimental import pallas as pl
from jax.experimental.pallas import tpu as pltpu

F32 = jnp.float32
BF16 = jnp.bfloat16

D_MODEL = 2048
GLA_H, GLA_DK, GLA_DV = 4, 64, 128
GLA_GATE_RANK = 16
GLA_GATE_TAU = 16.0
SSD_DINNER, SSD_HD, SSD_G, SSD_N, SSD_CONV = 512, 64, 2, 128, 4
SSD_H = SSD_DINNER // SSD_HD
SSD_CONV_DIM = SSD_DINNER + 2 * SSD_G * SSD_N
MLA_H, MLA_NOPE, MLA_ROPE, MLA_V = 4, 128, 64, 128
MLA_QK = MLA_NOPE + MLA_ROPE
MLA_Q_LORA, MLA_KV_LORA = 384, 128
DIL_H, DIL_HD = 4, 128
DIL_CONFIGS = ((128, 1), (512, 4), (2048, 16))
ROPE_THETA = 10000.0
EPS = 1e-6
LOG2E = math.log2(math.e)
D_MIX = GLA_H * GLA_DV + SSD_DINNER + MLA_H * MLA_V + DIL_H * DIL_HD

LANE = 128
SUBLANE = 8
MXU_DIM = 256
VMEM_REQUEST = 52 * 1024 * 1024
MIX_W = 512
MLA_PITCH = 2 * LANE

COL_GA, COL_GB, COL_GC, COL_GD = 0, 512, 1024, 1536
COL_AQK = 2048
COL_AV = 2560
COL_XBC = 3072
COL_CQKV = 4096
COL_DQ, COL_DK, COL_DV = 4608, 5120, 5632
COL_SMALL = 6144
COL_END = COL_SMALL + LANE
INPROJ_TN = 5 * MXU_DIM
N_PACK = -(-COL_END // INPROJ_TN) * INPROJ_TN
SM_KR, SM_LR, SM_DT = 0, 64, 80

GLA_CHUNK = 32
GLA_ROWS = 1024
GLA_BLOCK = 512
SSD_CHUNK = 256
SSD_ROWS = 512
DIL_BLK = 128
DIL_UNROLL = 16
DIL_DIRECT_STRIDE = 4
DIL_PIECE = 512
DIL_MERGE_ROWS = 256
INPROJ_TM = 1024
OUTPROJ_TM = 512
PREP_ROWS = 512
FLASH_TILE = 1024
PACK_COLS = 256
PACK_PIECE = 512


def _cparams(sem):
    return pltpu.CompilerParams(dimension_semantics=sem, vmem_limit_bytes=VMEM_REQUEST)


def _silu(x):
    return x * (1.0 / (1.0 + jnp.exp(-x)))


def _dot(a, b):
    return jnp.dot(a, b, preferred_element_type=F32)


def _dot_nt(a, b):
    return lax.dot_general(a, b, (((1,), (1,)), ((), ())), preferred_element_type=F32)


def _dot_tn(a, b):
    return lax.dot_general(a, b, (((0,), (0,)), ((), ())), preferred_element_type=F32)


def _split_dot(tri, x):
    hi = x.astype(BF16)
    lo = (x - hi.astype(F32)).astype(BF16)
    return _dot(tri, hi) + _dot(tri, lo)


def _rope_tables_kernel(pos_ref, f_ref, cm_ref, s1_ref, s2_ref, cd_ref, sd_ref):
    half = LANE // 2
    ang = pos_ref[...] * f_ref[...]
    cos, sin = jnp.cos(ang), jnp.sin(ang)
    lane = lax.broadcasted_iota(jnp.int32, (1, LANE), 1)
    low = lane < half
    cm_ref[...] = jnp.where(low, cos, 0.0)
    s1_ref[...] = jnp.where(lane < MLA_ROPE // 2, -sin, 0.0)
    s2_ref[...] = jnp.where((lane >= MLA_ROPE // 2) & low, sin, 0.0)
    cd_ref[...] = jnp.where(low, pltpu.roll(cos, half, 1), cos)
    sd_ref[...] = jnp.where(low, -pltpu.roll(sin, half, 1), sin)


def _rope_tables(positions):
    assert MLA_ROPE == LANE // 2 and DIL_HD == LANE
    T = positions.size
    ts = min(T, 2048)
    pos = positions.reshape(T, 1).astype(F32)
    lane = np.arange(LANE // 2)
    fm = np.exp(-math.log(ROPE_THETA) * (lane % (MLA_ROPE // 2)) * (2.0 / MLA_ROPE))
    fd = np.exp(-math.log(ROPE_THETA) * lane * (2.0 / DIL_HD))
    freqs = jnp.asarray(np.concatenate([fm, fd]), F32).reshape(1, LANE)
    row = pl.BlockSpec((ts, LANE), lambda i: (i, 0))
    return pl.pallas_call(
        _rope_tables_kernel,
        grid=(T // ts,),
        in_specs=[pl.BlockSpec((ts, 1), lambda i: (i, 0)), pl.BlockSpec((1, LANE), lambda i: (0, 0))],
        out_specs=[row] * 5,
        out_shape=[jax.ShapeDtypeStruct((T, LANE), F32)] * 5,
        compiler_params=_cparams(("parallel",)),
        name="rope_tables",
    )(pos, freqs)


def _inproj_kernel(x_ref, g_ref, w_ref, o_ref, h_ref):
    @pl.when(pl.program_id(1) == 0)
    def _():
        x = x_ref[...]
        ms = jnp.mean(x * x, axis=-1, keepdims=True)
        h_ref[...] = (x * lax.rsqrt(ms + EPS) * g_ref[...]).astype(BF16)

    o_ref[...] = _dot(h_ref[...], w_ref[...])


def _inproj(x, g, w, layer):
    T = x.shape[0]
    tm, tn = min(INPROJ_TM, T), INPROJ_TN
    return pl.pallas_call(
        _inproj_kernel,
        grid=(T // tm, N_PACK // tn),
        in_specs=[
            pl.BlockSpec((tm, D_MODEL), lambda i, j: (i, 0)),
            pl.BlockSpec((1, D_MODEL), lambda i, j: (0, 0)),
            pl.BlockSpec((None, D_MODEL, tn), lambda i, j: (layer, 0, j)),
        ],
        out_specs=pl.BlockSpec((tm, tn), lambda i, j: (i, j)),
        out_shape=jax.ShapeDtypeStruct((T, N_PACK), F32),
        scratch_shapes=[pltpu.VMEM((tm, D_MODEL), BF16)],
        compiler_params=_cparams(("parallel", "arbitrary")),
        name="inproj",
    )(x, g, w)


def _outproj_kernel(x_ref, ya_ref, yb_ref, yc_ref, yd_ref, w_ref, o_ref):
    acc = x_ref[...]
    for n, y_ref in enumerate((ya_ref, yb_ref, yc_ref, yd_ref)):
        acc = acc + _dot(y_ref[...], w_ref[n * MIX_W:(n + 1) * MIX_W, :])
    o_ref[...] = acc


def _outproj(x, ya, yb, yc, yd, w, layer):
    T = x.shape[0]
    tm = min(OUTPROJ_TM, T)
    yspec = pl.BlockSpec((tm, MIX_W), lambda i: (i, 0))
    return pl.pallas_call(
        _outproj_kernel,
        grid=(T // tm,),
        in_specs=[pl.BlockSpec((tm, D_MODEL), lambda i: (i, 0)), yspec, yspec, yspec, yspec,
                  pl.BlockSpec((None, D_MIX, D_MODEL), lambda i: (layer, 0, 0))],
        out_specs=pl.BlockSpec((tm, D_MODEL), lambda i: (i, 0)),
        out_shape=jax.ShapeDtypeStruct((T, D_MODEL), F32),
        compiler_params=_cparams(("parallel",)),
        name="outproj",
    )(x, ya, yb, yc, yd, w)


def _gla_kernel(qk_ref, v_ref, gate_ref, sm_ref, w2_ref, b2_ref, ng_ref, o_ref, st_ref):
    @pl.when(pl.program_id(1) == 0)
    def _():
        st_ref[...] = jnp.zeros_like(st_ref)

    R, RB, C = GLA_ROWS, GLA_BLOCK, GLA_CHUNK
    xg = _dot(sm_ref[...].astype(BF16), w2_ref[...]) + b2_ref[...]
    logd = (jnp.minimum(xg, 0.0) - jnp.log(1.0 + jnp.exp(-jnp.abs(xg)))) * (1.0 / GLA_GATE_TAU)
    ri = lax.broadcasted_iota(jnp.int32, (RB, RB), 0)
    ci = lax.broadcasted_iota(jnp.int32, (RB, RB), 1)
    back = ri - ci
    in_chunk = (back >= 0) & (back <= (ri & (C - 1)))
    cum = jnp.where(in_chunk, 1.0, 0.0).astype(BF16)
    ng = ng_ref[...]
    n_chunks = RB // C
    hk = GLA_H * GLA_DK
    wide = n_chunks * GLA_DK
    place = ((lax.broadcasted_iota(jnp.int32, (RB, wide), 0) >> (C.bit_length() - 1))
             == (lax.broadcasted_iota(jnp.int32, (RB, wide), 1) >> (GLA_DK.bit_length() - 1)))

    def block_diag(x):
        pair = jnp.concatenate([x, x], axis=1)
        return jnp.where(place, jnp.concatenate([pair] * (n_chunks // 2), axis=1), jnp.zeros((), x.dtype))

    for sb in range(R // RB):
        rows = slice(sb * RB, (sb + 1) * RB)
        bc = _split_dot(cum, logd[rows]).reshape(n_chunks, C, hk)
        b_mid = bc[:, C // 2 - 1:C // 2, :]
        b_last = bc[:, C - 1:C, :]
        q = (qk_ref[rows, :hk] * (GLA_DK ** -0.5)).reshape(n_chunks, C, hk)
        k = qk_ref[rows, hk:].reshape(n_chunks, C, hk)
        qd = (q * jnp.exp(bc - b_mid)).astype(BF16).reshape(RB, hk)
        kd = (k * jnp.exp(b_mid - bc)).astype(BF16).reshape(RB, hk)
        qe = (q * jnp.exp(bc)).astype(BF16).reshape(RB, hk)
        kl = (k * jnp.exp(b_last - bc)).astype(BF16).reshape(RB, hk)
        e_last = jnp.exp(b_last)
        for h in range(GLA_H):
            kcols = slice(h * GLA_DK, (h + 1) * GLA_DK)
            vcols = slice(h * GLA_DV, (h + 1) * GLA_DV)
            v = v_ref[rows, vcols].astype(BF16)
            attn = jnp.where(in_chunk, _dot_nt(qd[:, kcols], kd[:, kcols]), 0.0).astype(BF16)
            u_all = _dot_tn(v, block_diag(kl[:, kcols]))
            st = st_ref[h]
            entering = []
            for c in range(n_chunks):
                entering.append(st)
                st = st * e_last[c][:, kcols] + u_all[:, c * GLA_DK:(c + 1) * GLA_DK]
            st_ref[h] = st
            s_all = jnp.concatenate(entering, axis=1).astype(BF16)
            o = _dot(attn, v) + _dot_nt(block_diag(qe[:, kcols]), s_all)
            ms = jnp.mean(o * o, axis=-1, keepdims=True)
            y = o * lax.rsqrt(ms + EPS) * ng * _silu(gate_ref[rows, vcols])
            o_ref[rows, vcols] = y.astype(BF16)


def _gla(u, w2p, b2, ng, B, S):
    R = GLA_ROWS
    nb = S // R
    row = lambda c: (lambda b, i: (b * nb + i, c))
    const = lambda b, i: (0, 0)
    return pl.pallas_call(
        _gla_kernel,
        grid=(B, nb),
        in_specs=[
            pl.BlockSpec((R, MIX_W), row(COL_AQK // MIX_W)),
            pl.BlockSpec((R, MIX_W), row(COL_AV // MIX_W)),
            pl.BlockSpec((R, MIX_W), row(COL_GA // MIX_W)),
            pl.BlockSpec((R, LANE), row(COL_SMALL // LANE)),
            pl.BlockSpec((LANE, GLA_H * GLA_DK), const),
            pl.BlockSpec((1, GLA_H * GLA_DK), const),
            pl.BlockSpec((1, GLA_DV), const),
        ],
        out_specs=pl.BlockSpec((R, MIX_W), row(0)),
        out_shape=jax.ShapeDtypeStruct((B * S, GLA_H * GLA_DV), BF16),
        scratch_shapes=[pltpu.VMEM((GLA_H, GLA_DV, GLA_DK), F32)],
        compiler_params=_cparams(("parallel", "arbitrary")),
        name="gla",
    )(u, u, u, u, w2p, b2, ng)


def _ssd_kernel(xbc_ref, z_ref, sm_ref, cw_ref, cb_ref, dtb_ref, a_ref, d_ref, ng_ref, o_ref,
                xpad_ref, st_ref, y_ref):
    R, L = SSD_ROWS, SSD_CHUNK
    HALO = SUBLANE

    @pl.when(pl.program_id(1) == 0)
    def _():
        st_ref[...] = jnp.zeros_like(st_ref)
        xpad_ref[...] = jnp.zeros((HALO, SSD_CONV_DIM), F32)

    x = xbc_ref[...]
    halo = xpad_ref[...]
    row = lax.broadcasted_iota(jnp.int32, (HALO, 1), 0)
    conv = cb_ref[...] + x * cw_ref[SSD_CONV - 1:SSD_CONV, :]
    for k in range(1, SSD_CONV):
        shifted = pltpu.roll(x, k, 0)
        head = jnp.where(row < k, pltpu.roll(halo, k, 0), shifted[0:HALO])
        conv = conv + jnp.concatenate([head, shifted[HALO:]], axis=0) * cw_ref[SSD_CONV - 1 - k:SSD_CONV - k, :]
    xpad_ref[...] = x[R - HALO:R]
    xact = _silu(conv)

    lane = lax.broadcasted_iota(jnp.int32, (1, LANE), 1)
    is_dt = (lane >= SM_DT) & (lane < SM_DT + SSD_H)
    pre = sm_ref[...] + dtb_ref[...]
    dt_all = jnp.maximum(pre, 0.0) + jnp.log(1.0 + jnp.exp(-jnp.abs(pre)))
    a_all = jnp.where(is_dt, dt_all * -jnp.exp(a_ref[...]), 0.0)
    ri = lax.broadcasted_iota(jnp.int32, (L, L), 0)
    ci = lax.broadcasted_iota(jnp.int32, (L, L), 1)
    lower = ci <= ri
    cum = jnp.where(lower, 1.0, 0.0).astype(BF16)

    heads_per_group = SSD_H // SSD_G
    for sb in range(R // L):
        rows = slice(sb * L, (sb + 1) * L)
        cs_all = _split_dot(cum, a_all[rows])
        cs_t = cs_all.T
        for g in range(SSD_G):
            bm = xact[rows, SSD_DINNER + g * SSD_N:SSD_DINNER + (g + 1) * SSD_N].astype(BF16)
            cm = xact[rows, SSD_DINNER + SSD_G * SSD_N + g * SSD_N:
                      SSD_DINNER + SSD_G * SSD_N + (g + 1) * SSD_N].astype(BF16)
            scores = _dot_nt(cm, bm)
            for hh in range(heads_per_group):
                h = g * heads_per_group + hh
                hcols = slice(h * SSD_HD, (h + 1) * SSD_HD)
                xs = xact[rows, hcols]
                dt = dt_all[rows, SM_DT + h:SM_DT + h + 1]
                cs_col = cs_all[:, SM_DT + h:SM_DT + h + 1]
                cs_row = cs_t[SM_DT + h:SM_DT + h + 1, :]
                cs_last = cs_col[L - 1:L, :]
                decay = jnp.where(lower, jnp.exp(cs_col - cs_row), 0.0)
                xdt = xs * dt
                y = _dot((scores * decay).astype(BF16), xdt.astype(BF16))
                prev = st_ref[h]
                y = y + _dot_nt(cm, prev.astype(BF16)) * jnp.exp(cs_col)
                st_ref[h] = prev * jnp.exp(cs_last) + _dot_tn(
                    (xdt * jnp.exp(cs_last - cs_col)).astype(BF16), bm)
                y_ref[rows, hcols] = y + xs * d_ref[:, hcols]

    yz = y_ref[...] * _silu(z_ref[...])
    gw = SSD_DINNER // SSD_G
    for g in range(SSD_G):
        yg = yz[:, g * gw:(g + 1) * gw]
        ms = jnp.mean(yg * yg, axis=-1, keepdims=True)
        o_ref[:, g * gw:(g + 1) * gw] = (yg * lax.rsqrt(ms + EPS) * ng_ref[:, g * gw:(g + 1) * gw]).astype(BF16)


def _ssd(u, cw, cb, dtb, a_pad, d_exp, ng, B, S):
    L = SSD_ROWS
    nb = S // L
    row = lambda c: (lambda b, i: (b * nb + i, c))
    const = lambda b, i: (0, 0)
    return pl.pallas_call(
        _ssd_kernel,
        grid=(B, nb),
        in_specs=[
            pl.BlockSpec((L, SSD_CONV_DIM), row(COL_XBC // SSD_CONV_DIM)),
            pl.BlockSpec((L, MIX_W), row(COL_GB // MIX_W)),
            pl.BlockSpec((L, LANE), row(COL_SMALL // LANE)),
            pl.BlockSpec((SSD_CONV, SSD_CONV_DIM), const),
            pl.BlockSpec((1, SSD_CONV_DIM), const),
            pl.BlockSpec((1, LANE), const),
            pl.BlockSpec((1, LANE), const),
            pl.BlockSpec((1, SSD_DINNER), const),
            pl.BlockSpec((1, SSD_DINNER), const),
        ],
        out_specs=pl.BlockSpec((L, SSD_DINNER), row(0)),
        out_shape=jax.ShapeDtypeStruct((B * S, SSD_DINNER), BF16),
        scratch_shapes=[pltpu.VMEM((SUBLANE, SSD_CONV_DIM), F32),
                        pltpu.VMEM((SSD_H, SSD_HD, SSD_N), F32),
                        pltpu.VMEM((L, SSD_DINNER), F32)],
        compiler_params=_cparams(("parallel", "arbitrary")),
        name="ssd",
    )(u, u, u, cw, cb, dtb, a_pad, d_exp, ng)


def _rope64(x, cm, s1, s2):
    return x * cm + pltpu.roll(x, LANE - MLA_ROPE // 2, 1) * s1 + pltpu.roll(x, MLA_ROPE // 2, 1) * s2


def _mla_prep_kernel(c_ref, sm_ref, cm_ref, s1_ref, s2_ref, wq_ref, wkv_ref, qng_ref, kvng_ref,
                     qhg_ref, khg_ref, q_ref, k_ref, v_ref):
    blk = c_ref[...]
    cq = blk[:, :MLA_Q_LORA]
    ckv = blk[:, MLA_Q_LORA:]
    def lane_tile_sum(x):
        tiles = [x[:, c:c + LANE] for c in range(0, x.shape[1], LANE)]
        return jnp.sum(functools.reduce(lambda a, b: a + b, tiles), axis=-1, keepdims=True)

    cqn = cq * lax.rsqrt(lane_tile_sum(cq * cq) * (1.0 / MLA_Q_LORA) + EPS) * qng_ref[...]
    ckvn = ckv * lax.rsqrt(lane_tile_sum(ckv * ckv) * (1.0 / MLA_KV_LORA) + EPS) * kvng_ref[...]
    qf = _dot(cqn.astype(BF16), wq_ref[...])
    kvf = _dot(ckvn.astype(BF16), wkv_ref[...])
    cm, s1, s2 = cm_ref[...], s1_ref[...], s2_ref[...]
    lane = lax.broadcasted_iota(jnp.int32, (1, LANE), 1)
    kr = jnp.where(lane < MLA_ROPE, sm_ref[...], 0.0)
    kr_sq = kr * kr
    qhg, khg = qhg_ref[...], khg_ref[...]
    kr_rot = _rope64(kr * khg[:, LANE:], cm, s1, s2)
    for h in range(MLA_H):
        qh = qf[:, MLA_PITCH * h:MLA_PITCH * (h + 1)]
        rq = lax.rsqrt(lane_tile_sum(qh * qh) * (1.0 / MLA_QK) + EPS)
        rq = rq * (MLA_QK ** -0.5 * LOG2E)
        q_ref[0, h, :, 0:LANE] = (qh[:, :LANE] * rq * qhg[:, :LANE]).astype(BF16)
        q_ref[0, h, :, LANE:2 * LANE] = _rope64(qh[:, LANE:] * rq * qhg[:, LANE:], cm, s1, s2).astype(BF16)
        kn = kvf[:, MLA_PITCH * h:MLA_PITCH * h + LANE]
        rk = lax.rsqrt(jnp.sum(kn * kn + kr_sq, axis=-1, keepdims=True) * (1.0 / MLA_QK) + EPS)
        k_ref[0, h, :, 0:LANE] = (kn * rk * khg[:, :LANE]).astype(BF16)
        k_ref[0, h, :, LANE:2 * LANE] = (kr_rot * rk).astype(BF16)
        v_ref[0, h, :, 0:MLA_V] = kvf[:, MLA_PITCH * h + LANE:MLA_PITCH * (h + 1)].astype(BF16)
        v_ref[0, h, :, MLA_V:2 * MLA_V] = jnp.ones((kvf.shape[0], MLA_V), BF16)


def _mla_prep(u, cm, s1, s2, wq, wkv, qng, kvng, qhg, khg, B, S):
    ts = min(PREP_ROWS, S)
    nb = S // ts
    row = lambda c: (lambda b, i: (b * nb + i, c))
    const = lambda b, i: (0, 0)
    hspec = lambda w: pl.BlockSpec((1, MLA_H, ts, w), lambda b, i: (b, 0, i, 0))
    return pl.pallas_call(
        _mla_prep_kernel,
        grid=(B, nb),
        in_specs=[
            pl.BlockSpec((ts, MIX_W), row(COL_CQKV // MIX_W)),
            pl.BlockSpec((ts, LANE), row(COL_SMALL // LANE)),
            pl.BlockSpec((ts, LANE), row(0)), pl.BlockSpec((ts, LANE), row(0)), pl.BlockSpec((ts, LANE), row(0)),
            pl.BlockSpec((MLA_Q_LORA, MLA_H * MLA_PITCH), const),
            pl.BlockSpec((MLA_KV_LORA, MLA_H * MLA_PITCH), const),
            pl.BlockSpec((1, MLA_Q_LORA), const),
            pl.BlockSpec((1, MLA_KV_LORA), const),
            pl.BlockSpec((1, MLA_PITCH), const),
            pl.BlockSpec((1, MLA_PITCH), const),
        ],
        out_specs=[hspec(MLA_PITCH), hspec(MLA_PITCH), hspec(2 * MLA_V)],
        out_shape=[jax.ShapeDtypeStruct((B, MLA_H, S, MLA_PITCH), BF16),
                   jax.ShapeDtypeStruct((B, MLA_H, S, MLA_PITCH), BF16),
                   jax.ShapeDtypeStruct((B, MLA_H, S, 2 * MLA_V), BF16)],
        compiler_params=_cparams(("parallel", "parallel")),
        name="mla_prep",
    )(u, u, cm, s1, s2, wq, wkv, qng, kvng, qhg, khg)


def _flash_kernel(q_ref, k_ref, v_ref, g_ref, o_ref, m_ref, acc_ref, sa_ref, sb_ref, *, t):
    i = pl.program_id(2)
    m_ref[...] = jnp.full_like(m_ref, -jnp.inf)
    acc_ref[...] = jnp.zeros_like(acc_ref)
    q = q_ref[0, 0]

    def rows(j):
        return pl.ds(pl.multiple_of(j * t, t), t)

    def scores(j):
        return _dot_nt(q, k_ref[0, 0, rows(j), :])

    def update(s, j, qrows=slice(0, t)):
        nk = s.shape[1]
        m_old = m_ref[qrows]
        m_new = jnp.maximum(m_old, jnp.max(s, axis=-1, keepdims=True))
        alpha = jnp.exp2(m_old - m_new)
        p = jnp.exp2(s - jnp.concatenate([m_new] * (nk // LANE), axis=1))
        vrows = pl.ds(pl.multiple_of(j * t, t), nk)
        acc_ref[qrows] = (jnp.concatenate([alpha, alpha], axis=1) * acc_ref[qrows]
                          + _dot(p.astype(BF16), v_ref[0, 0, vrows, :]))
        m_ref[qrows] = m_new

    def diagonal(s_ref):
        h = t // 2
        top = lax.broadcasted_iota(jnp.int32, (h, h), 1) <= lax.broadcasted_iota(jnp.int32, (h, h), 0)
        update(jnp.where(top, s_ref[0:h, 0:h], -jnp.inf), i, slice(0, h))
        low = lax.broadcasted_iota(jnp.int32, (h, t), 1) <= lax.broadcasted_iota(jnp.int32, (h, t), 0) + h
        update(jnp.where(low, s_ref[h:t, :], -jnp.inf), i, slice(h, t))

    sa_ref[...] = scores(0)

    def pair(p, carry):
        j = 2 * p
        s = sa_ref[...]
        sb_ref[...] = scores(j + 1)
        update(s, j)
        s = sb_ref[...]
        sa_ref[...] = scores(j + 2)
        update(s, j + 1)
        return carry

    lax.fori_loop(0, i // 2, pair, 0)

    @pl.when(i % 2 == 1)
    def _():
        s = sa_ref[...]
        sb_ref[...] = scores(i)
        update(s, i - 1)
        diagonal(sb_ref)

    @pl.when(i % 2 == 0)
    def _():
        diagonal(sa_ref)

    o_ref[...] = (acc_ref[:, :MLA_V] / acc_ref[:, MLA_V:] * _silu(g_ref[...])).astype(BF16)


def _flash(q, k, v, u, B, S):
    t = min(FLASH_TILE, S)
    nq = S // t
    return pl.pallas_call(
        functools.partial(_flash_kernel, t=t),
        grid=(B, MLA_H, nq),
        in_specs=[
            pl.BlockSpec((1, 1, t, MLA_PITCH), lambda b, h, i: (b, h, i, 0)),
            pl.BlockSpec((1, 1, S, MLA_PITCH), lambda b, h, i: (b, h, 0, 0)),
            pl.BlockSpec((1, 1, S, 2 * MLA_V), lambda b, h, i: (b, h, 0, 0)),
            pl.BlockSpec((t, LANE), lambda b, h, i: (b * nq + i, COL_GC // LANE + h)),
        ],
        out_specs=pl.BlockSpec((t, LANE), lambda b, h, i: (b * nq + i, h)),
        out_shape=jax.ShapeDtypeStruct((B * S, MLA_H * MLA_V), BF16),
        scratch_shapes=[pltpu.VMEM((t, LANE), F32), pltpu.VMEM((t, 2 * MLA_V), F32),
                        pltpu.VMEM((t, t), F32), pltpu.VMEM((t, t), F32)],
        compiler_params=_cparams(("parallel", "parallel", "arbitrary")),
        name="mla_flash",
    )(q, k, v, u)


def _dil_kernel(q_ref, k_ref, v_ref, g_ref, cd_ref, sd_ref, qg_ref, kg_ref, o_ref,
                qs_ref, ks_ref, qd_ref, kd_ref, vd_ref, oc_ref, ec_ref, tmp_ref, *, S):
    Q = DIL_BLK
    cd, sd = cd_ref[...], sd_ref[...]

    def norm_rope(x, g):
        xn = x * lax.rsqrt(jnp.mean(x * x, axis=-1, keepdims=True) + EPS) * g
        return xn * cd + pltpu.roll(xn, DIL_HD // 2, 1) * sd

    qs_ref[...] = norm_rope(q_ref[...], qg_ref[...] * (DIL_HD ** -0.5 * LOG2E))
    ks_ref[...] = norm_rope(k_ref[...], kg_ref[...])

    ra = lax.broadcasted_iota(jnp.int32, (Q, 2 * Q), 0)
    cc = lax.broadcasted_iota(jnp.int32, (Q, 2 * Q), 1)
    band = (cc >= ra) & (cc <= ra + Q)
    bias_inner = jnp.where(band, 0.0, -jnp.inf)
    bias_first = jnp.where(band & (cc >= Q), 0.0, -jnp.inf)
    n_units = S // Q
    vd_vals = vd_ref.at[:, 0:DIL_HD]
    vd_ref[:, DIL_HD:] = jnp.ones((vd_ref.shape[0], DIL_HD), BF16)

    for c, (w, d) in enumerate(DIL_CONFIGS):
        assert w // d == Q
        n_sub = S // d
        pitch = n_sub + Q
        nblk = n_sub // Q
        piece = min(n_sub, DIL_PIECE)
        streams = ((qs_ref, qd_ref, n_sub, 0), (ks_ref, kd_ref, pitch, Q), (v_ref, vd_vals, pitch, Q))
        for r in range(d):
            kd_ref[r * pitch:r * pitch + Q, :] = jnp.zeros((Q, DIL_HD), BF16)
            vd_vals[r * pitch:r * pitch + Q, :] = jnp.zeros((Q, DIL_HD), BF16)
        if d <= DIL_DIRECT_STRIDE:
            for r in range(d):
                for c0 in range(0, n_sub, piece):
                    src = pl.ds(r + c0 * d, piece, stride=d)
                    for src_ref, dst_ref, cpitch, lead in streams:
                        dst = r * cpitch + lead + c0
                        dst_ref[dst:dst + piece, :] = src_ref[src, :].astype(BF16)
        else:
            outer, inner = DIL_DIRECT_STRIDE, d // DIL_DIRECT_STRIDE
            n_outer = S // outer
            assert d % outer == 0 and inner <= DIL_DIRECT_STRIDE
            for r0 in range(outer):
                for src_ref, dst_ref, cpitch, lead in streams:
                    step = min(n_outer, DIL_PIECE)
                    for c0 in range(0, n_outer, step):
                        tmp_ref[c0:c0 + step, :] = src_ref[pl.ds(r0 + c0 * outer, step, stride=outer), :]
                    for m in range(inner):
                        dst = (r0 + outer * m) * cpitch + lead
                        dst_ref[dst:dst + n_sub, :] = tmp_ref[pl.ds(m, n_sub, stride=inner), :].astype(BF16)

        unroll = min(DIL_UNROLL, n_units)

        def group(n0, carry, c=c, n_sub=n_sub, pitch=pitch, nblk=nblk, unroll=unroll):
            for uu in range(unroll):
                n = n0 * unroll + uu
                r = n // nblk
                i = n - r * nblk
                qrows = pl.ds(pl.multiple_of(r * n_sub + i * Q, Q), Q)
                krows = pl.ds(pl.multiple_of(r * pitch + i * Q, Q), 2 * Q)
                s = _dot_nt(qd_ref[qrows, :], kd_ref[krows, :]) + jnp.where(i > 0, bias_inner, bias_first)
                m = jnp.max(s, axis=-1, keepdims=True)
                p = jnp.exp2(s - m)
                ol = _dot(p.astype(BF16), vd_ref[krows, :])
                l = ol[:, DIL_HD:]
                nat = pl.ds(i * (d * Q) + r, Q, stride=d) if d > 1 else pl.ds(pl.multiple_of(i * Q, Q), Q)
                oc_ref[c, nat, :] = ol[:, :DIL_HD] * (1.0 / l)
                ec_ref[c, nat, :] = m + jnp.log2(l)
            return carry

        lax.fori_loop(0, n_units // unroll, group, 0)

    rows_per_step = min(S, DIL_MERGE_ROWS)

    def merge(n, carry):
        rows = pl.ds(pl.multiple_of(n * rows_per_step, rows_per_step), rows_per_step)
        es = [ec_ref[c, rows, :] for c in range(len(DIL_CONFIGS))]
        e_max = functools.reduce(jnp.maximum, es)
        ws = [jnp.exp2(e - e_max) for e in es]
        num = sum(wt * oc_ref[c, rows, :] for c, wt in enumerate(ws))
        o_ref[rows, :] = (num / sum(ws) * _silu(g_ref[rows, :])).astype(BF16)
        return carry

    lax.fori_loop(0, S // rows_per_step, merge, 0)


def _dil(u, cd, sd, qg, kg, B, S):
    col = lambda c: (lambda b, h: (b, c // LANE + h))
    const = lambda b, h: (0, 0)
    blk = lambda im: pl.BlockSpec((S, LANE), im)
    kv_rows = max(S + d * DIL_BLK for _, d in DIL_CONFIGS)
    table = pl.BlockSpec((S, LANE), lambda b, h: (b, 0), pipeline_mode=pl.Buffered(1))
    return pl.pallas_call(
        functools.partial(_dil_kernel, S=S),
        grid=(B, DIL_H),
        in_specs=[blk(col(COL_DQ)), blk(col(COL_DK)), blk(col(COL_DV)), blk(col(COL_GD)),
                  table, table,
                  pl.BlockSpec((1, DIL_HD), const), pl.BlockSpec((1, DIL_HD), const)],
        out_specs=blk(lambda b, h: (b, h)),
        out_shape=jax.ShapeDtypeStruct((B * S, DIL_H * DIL_HD), BF16),
        scratch_shapes=[pltpu.VMEM((S, DIL_HD), F32), pltpu.VMEM((S, DIL_HD), F32),
                        pltpu.VMEM((S, DIL_HD), BF16),
                        pltpu.VMEM((kv_rows, DIL_HD), BF16), pltpu.VMEM((kv_rows, 2 * DIL_HD), BF16),
                        pltpu.VMEM((len(DIL_CONFIGS), S, DIL_HD), F32),
                        pltpu.VMEM((len(DIL_CONFIGS), S, LANE), F32),
                        pltpu.VMEM((S // DIL_DIRECT_STRIDE, DIL_HD), F32)],
        compiler_params=_cparams(("parallel", "parallel")),
        name="dilated",
    )(u, u, u, u, cd, sd, qg, kg)


SRC_ALR, SRC_XBC, SRC_DT, SRC_CQKV, SRC_KR, SRC_DQKV = 3072, 3088, 4112, 4120, 4632, 4696
N_IN = 6232
PACK_RUNS = ((0, COL_XBC, 0), (SRC_XBC, SSD_CONV_DIM, COL_XBC), (SRC_CQKV, MLA_Q_LORA + MLA_KV_LORA, COL_CQKV),
             (SRC_DQKV, 3 * MIX_W, COL_DQ))


def _pack_kernel(wt_ref, o_ref):
    tc = wt_ref.shape[2]
    for src, width, dst in PACK_RUNS:
        for c0 in range(0, width, PACK_PIECE):
            o_ref[0, :, dst + c0:dst + c0 + PACK_PIECE] = (
                wt_ref[0, src + c0:src + c0 + PACK_PIECE, :].T.astype(BF16))
    small = jnp.concatenate(
        [wt_ref[0, SRC_KR:SRC_KR + MLA_ROPE, :], wt_ref[0, SRC_ALR:SRC_ALR + GLA_GATE_RANK, :],
         wt_ref[0, SRC_DT:SRC_DT + SSD_H, :], jnp.zeros((LANE - SM_DT - SSD_H, tc), F32)], axis=0)
    o_ref[0, :, COL_SMALL:COL_END] = small.T.astype(BF16)
    o_ref[0, :, COL_END:N_PACK] = jnp.zeros((tc, N_PACK - COL_END), BF16)


def _pack_w_in(w):
    depth, kdim, n_in = w.shape
    assert n_in == N_IN and kdim % PACK_COLS == 0 and all(width % PACK_PIECE == 0 for _, width, _ in PACK_RUNS)
    return pl.pallas_call(
        _pack_kernel,
        grid=(depth, kdim // PACK_COLS),
        in_specs=[pl.BlockSpec((1, N_IN, PACK_COLS), lambda l, i: (l, 0, i))],
        out_specs=pl.BlockSpec((1, PACK_COLS, N_PACK), lambda l, i: (l, i, 0)),
        out_shape=jax.ShapeDtypeStruct((depth, kdim, N_PACK), BF16),
        compiler_params=_cparams(("parallel", "parallel")),
        name="pack_w_in",
    )(jnp.swapaxes(w, 1, 2))


def _row(v, width=None, offset=0):
    v = v.astype(F32).reshape(1, -1)
    if width is None:
        return v
    return jnp.pad(v, ((0, 0), (offset, width - offset - v.shape[1])))


def _pad_heads(w, n_heads, real, padded):
    k = w.shape[0]
    w = w.reshape(k, n_heads, real)
    return jnp.pad(w, ((0, 0), (0, 0), (0, padded - real))).reshape(k, n_heads * padded)


def kernel(x, positions, ln_g, w_in, w_out, gla_gate_w2, gla_gate_b, gla_norm_g, ssd_conv_w, ssd_conv_b,
           ssd_dt_bias, ssd_A_log, ssd_D, ssd_norm_g, mla_q_norm_g, mla_kv_norm_g, mla_w_uq, mla_w_ukv,
           mla_q_head_g, mla_k_head_g, dil_q_g, dil_k_g):
    B, S, D = x.shape
    depth = w_in.shape[0]
    assert D == D_MODEL and S % SSD_ROWS == 0 and S % GLA_ROWS == 0
    assert all(S % (d * DIL_BLK) == 0 for _, d in DIL_CONFIGS)
    T = B * S
    cm, s1, s2, cd, sd = _rope_tables(positions)
    w_in_packed = _pack_w_in(w_in)
    w_out_bf16 = w_out.astype(BF16)
    xf = x.reshape(T, D)
    for l in range(depth):
        u = _inproj(xf, _row(ln_g[l]), w_in_packed, l)
        w2p = jnp.pad(gla_gate_w2[l], ((SM_LR, LANE - SM_LR - GLA_GATE_RANK), (0, 0))).astype(BF16)
        ya = _gla(u, w2p, _row(gla_gate_b[l]), _row(gla_norm_g[l]), B, S)
        yb = _ssd(u, ssd_conv_w[l].astype(F32), _row(ssd_conv_b[l]),
                  _row(ssd_dt_bias[l], LANE, SM_DT), _row(ssd_A_log[l], LANE, SM_DT),
                  _row(jnp.repeat(ssd_D[l], SSD_HD)), _row(ssd_norm_g[l]), B, S)
        q, k, v = _mla_prep(u, cm, s1, s2,
                            _pad_heads(mla_w_uq[l], MLA_H, MLA_QK, MLA_PITCH).astype(BF16),
                            mla_w_ukv[l].astype(BF16), _row(mla_q_norm_g[l]), _row(mla_kv_norm_g[l]),
                            _row(mla_q_head_g[l], MLA_PITCH), _row(mla_k_head_g[l], MLA_PITCH), B, S)
        yc = _flash(q, k, v, u, B, S)
        yd = _dil(u, cd, sd, _row(dil_q_g[l]), _row(dil_k_g[l]), B, S)
        xf = _outproj(xf, ya, yb, yc, yd, w_out_bf16, l)
    return xf.reshape(B, S, D)
```

```python
import functools
import math

import jax
import jax.numpy as jnp
import numpy as np
from jax import lax
from jax.experimental import pallas as pl
from jax.experimental.pallas import tpu as pltpu

F32 = jnp.float32
BF16 = jnp.bfloat16

D_MODEL = 2048
GLA_H, GLA_DK, GLA_DV = 4, 64, 128
GLA_GATE_RANK = 16
GLA_GATE_TAU = 16.0
SSD_DINNER, SSD_HD, SSD_G, SSD_N, SSD_CONV = 512, 64, 2, 128, 4
SSD_H = SSD_DINNER // SSD_HD
SSD_CONV_DIM = SSD_DINNER + 2 * SSD_G * SSD_N
MLA_H, MLA_NOPE, MLA_ROPE, MLA_V = 4, 128, 64, 128
MLA_QK = MLA_NOPE + MLA_ROPE
MLA_Q_LORA, MLA_KV_LORA = 384, 128
DIL_H, DIL_HD = 4, 128
DIL_CONFIGS = ((128, 1), (512, 4), (2048, 16))
ROPE_THETA = 10000.0
EPS = 1e-6
LOG2E = math.log2(math.e)
D_MIX = GLA_H * GLA_DV + SSD_DINNER + MLA_H * MLA_V + DIL_H * DIL_HD

LANE = 128
SUBLANE = 8
MXU_DIM = 256
VMEM_REQUEST = 52 * 1024 * 1024
MIX_W = 512
MLA_PITCH = 2 * LANE

COL_GA, COL_GB, COL_GC, COL_GD = 0, 512, 1024, 1536
COL_AQK = 2048
COL_AV = 2560
COL_XBC = 3072
COL_CQKV = 4096
COL_DQ, COL_DK, COL_DV = 4608, 5120, 5632
COL_SMALL = 6144
COL_END = COL_SMALL + LANE
INPROJ_TN = 5 * MXU_DIM
N_PACK = -(-COL_END // INPROJ_TN) * INPROJ_TN
SM_KR, SM_LR, SM_DT = 0, 64, 80

GLA_CHUNK = 32
GLA_ROWS = 1024
GLA_BLOCK = 512
SSD_CHUNK = 256
SSD_ROWS = 512
DIL_BLK = 128
DIL_UNROLL = 16
DIL_DIRECT_STRIDE = 4
DIL_PIECE = 512
DIL_MERGE_ROWS = 256
INPROJ_TM = 1024
INPROJ_PAIR = 2
OUTPROJ_TM = 512
PREP_ROWS = 512
FLASH_TILE = 1024
PACK_COLS = 256
PACK_PIECE = 512


def _cparams(sem):
    return pltpu.CompilerParams(dimension_semantics=sem, vmem_limit_bytes=VMEM_REQUEST)


def _silu(x):
    return x * (1.0 / (1.0 + jnp.exp(-x)))


def _dot(a, b):
    return jnp.dot(a, b, preferred_element_type=F32)


def _dot_nt(a, b):
    return lax.dot_general(a, b, (((1,), (1,)), ((), ())), preferred_element_type=F32)


def _dot_tn(a, b):
    return lax.dot_general(a, b, (((0,), (0,)), ((), ())), preferred_element_type=F32)


def _split_dot(tri, x):
    hi = x.astype(BF16)
    lo = (x - hi.astype(F32)).astype(BF16)
    return _dot(tri, hi) + _dot(tri, lo)


def _rope_tables_kernel(pos_ref, f_ref, cm_ref, s1_ref, s2_ref, cd_ref, sd_ref):
    half = LANE // 2
    ang = pos_ref[...] * f_ref[...]
    cos, sin = jnp.cos(ang), jnp.sin(ang)
    lane = lax.broadcasted_iota(jnp.int32, (1, LANE), 1)
    low = lane < half
    cm_ref[...] = jnp.where(low, cos, 0.0)
    s1_ref[...] = jnp.where(lane < MLA_ROPE // 2, -sin, 0.0)
    s2_ref[...] = jnp.where((lane >= MLA_ROPE // 2) & low, sin, 0.0)
    cd_ref[...] = jnp.where(low, pltpu.roll(cos, half, 1), cos)
    sd_ref[...] = jnp.where(low, -pltpu.roll(sin, half, 1), sin)


def _rope_tables(positions):
    assert MLA_ROPE == LANE // 2 and DIL_HD == LANE
    T = positions.size
    ts = min(T, 2048)
    pos = positions.reshape(T, 1).astype(F32)
    lane = np.arange(LANE // 2)
    fm = np.exp(-math.log(ROPE_THETA) * (lane % (MLA_ROPE // 2)) * (2.0 / MLA_ROPE))
    fd = np.exp(-math.log(ROPE_THETA) * lane * (2.0 / DIL_HD))
    freqs = jnp.asarray(np.concatenate([fm, fd]), F32).reshape(1, LANE)
    row = pl.BlockSpec((ts, LANE), lambda i: (i, 0))
    return pl.pallas_call(
        _rope_tables_kernel,
        grid=(T // ts,),
        in_specs=[pl.BlockSpec((ts, 1), lambda i: (i, 0)), pl.BlockSpec((1, LANE), lambda i: (0, 0))],
        out_specs=[row] * 5,
        out_shape=[jax.ShapeDtypeStruct((T, LANE), F32)] * 5,
        compiler_params=_cparams(("parallel",)),
        name="rope_tables",
    )(pos, freqs)


def _inproj_kernel(x_ref, g_ref, w_ref, o_ref, h_ref):
    r = pl.program_id(2)

    @pl.when(pl.program_id(1) == 0)
    def _():
        x = x_ref[...]
        ms = jnp.mean(x * x, axis=-1, keepdims=True)
        h_ref[r] = (x * lax.rsqrt(ms + EPS) * g_ref[...]).astype(BF16)

    o_ref[...] = _dot(h_ref[r], w_ref[...])


def _inproj(x, g, w, layer):
    T = x.shape[0]
    tm, tn = min(INPROJ_TM, T), INPROJ_TN
    pair = INPROJ_PAIR if T % (INPROJ_PAIR * tm) == 0 else 1
    last = pair - 1
    return pl.pallas_call(
        _inproj_kernel,
        grid=(T // (pair * tm), N_PACK // tn, pair),
        in_specs=[
            pl.BlockSpec((tm, D_MODEL), lambda i, j, r: (pair * i + jnp.where(j == 0, r, last), 0)),
            pl.BlockSpec((1, D_MODEL), lambda i, j, r: (0, 0)),
            pl.BlockSpec((None, D_MODEL, tn), lambda i, j, r: (layer, 0, j)),
        ],
        out_specs=pl.BlockSpec((tm, tn), lambda i, j, r: (pair * i + r, j)),
        out_shape=jax.ShapeDtypeStruct((T, N_PACK), F32),
        scratch_shapes=[pltpu.VMEM((pair, tm, D_MODEL), BF16)],
        compiler_params=_cparams(("parallel", "arbitrary", "arbitrary")),
        name="inproj",
    )(x, g, w)


def _outproj_kernel(x_ref, ya_ref, yb_ref, yc_ref, yd_ref, w_ref, o_ref):
    acc = x_ref[...]
    for n, y_ref in enumerate((ya_ref, yb_ref, yc_ref, yd_ref)):
        acc = acc + _dot(y_ref[...], w_ref[n * MIX_W:(n + 1) * MIX_W, :])
    o_ref[...] = acc


def _outproj(x, ya, yb, yc, yd, w, layer):
    T = x.shape[0]
    tm = min(OUTPROJ_TM, T)
    yspec = pl.BlockSpec((tm, MIX_W), lambda i: (i, 0))
    return pl.pallas_call(
        _outproj_kernel,
        grid=(T // tm,),
        in_specs=[pl.BlockSpec((tm, D_MODEL), lambda i: (i, 0)), yspec, yspec, yspec, yspec,
                  pl.BlockSpec((None, D_MIX, D_MODEL), lambda i: (layer, 0, 0))],
        out_specs=pl.BlockSpec((tm, D_MODEL), lambda i: (i, 0)),
        out_shape=jax.ShapeDtypeStruct((T, D_MODEL), F32),
        compiler_params=_cparams(("parallel",)),
        name="outproj",
    )(x, ya, yb, yc, yd, w)


def _gla_kernel(qk_ref, v_ref, gate_ref, sm_ref, w2_ref, b2_ref, ng_ref, o_ref, st_ref):
    @pl.when(pl.program_id(1) == 0)
    def _():
        st_ref[...] = jnp.zeros_like(st_ref)

    R, RB, C = GLA_ROWS, GLA_BLOCK, GLA_CHUNK
    xg = _dot(sm_ref[...].astype(BF16), w2_ref[...]) + b2_ref[...]
    logd = (jnp.minimum(xg, 0.0) - jnp.log(1.0 + jnp.exp(-jnp.abs(xg)))) * (1.0 / GLA_GATE_TAU)
    ri = lax.broadcasted_iota(jnp.int32, (RB, RB), 0)
    ci = lax.broadcasted_iota(jnp.int32, (RB, RB), 1)
    back = ri - ci
    in_chunk = (back >= 0) & (back <= (ri & (C - 1)))
    cum = jnp.where(in_chunk, 1.0, 0.0).astype(BF16)
    ng = ng_ref[...]
    n_chunks = RB // C
    hk = GLA_H * GLA_DK
    wide = n_chunks * GLA_DK
    place = ((lax.broadcasted_iota(jnp.int32, (RB, wide), 0) >> (C.bit_length() - 1))
             == (lax.broadcasted_iota(jnp.int32, (RB, wide), 1) >> (GLA_DK.bit_length() - 1)))

    def block_diag(x):
        pair = jnp.concatenate([x, x], axis=1)
        return jnp.where(place, jnp.concatenate([pair] * (n_chunks // 2), axis=1), jnp.zeros((), x.dtype))

    for sb in range(R // RB):
        rows = slice(sb * RB, (sb + 1) * RB)
        bc = _split_dot(cum, logd[rows]).reshape(n_chunks, C, hk)
        b_mid = bc[:, C // 2 - 1:C // 2, :]
        b_last = bc[:, C - 1:C, :]
        q = (qk_ref[rows, :hk] * (GLA_DK ** -0.5)).reshape(n_chunks, C, hk)
        k = qk_ref[rows, hk:].reshape(n_chunks, C, hk)
        qd = (q * jnp.exp(bc - b_mid)).astype(BF16).reshape(RB, hk)
        kd = (k * jnp.exp(b_mid - bc)).astype(BF16).reshape(RB, hk)
        qe = (q * jnp.exp(bc)).astype(BF16).reshape(RB, hk)
        kl = (k * jnp.exp(b_last - bc)).astype(BF16).reshape(RB, hk)
        e_last = jnp.exp(b_last)
        for h in range(GLA_H):
            kcols = slice(h * GLA_DK, (h + 1) * GLA_DK)
            vcols = slice(h * GLA_DV, (h + 1) * GLA_DV)
            v = v_ref[rows, vcols].astype(BF16)
            attn = jnp.where(in_chunk, _dot_nt(qd[:, kcols], kd[:, kcols]), 0.0).astype(BF16)
            u_all = _dot_tn(v, block_diag(kl[:, kcols]))
            st = st_ref[h]
            entering = []
            for c in range(n_chunks):
                entering.append(st)
                st = st * e_last[c][:, kcols] + u_all[:, c * GLA_DK:(c + 1) * GLA_DK]
            st_ref[h] = st
            s_all = jnp.concatenate(entering, axis=1).astype(BF16)
            o = _dot(attn, v) + _dot_nt(block_diag(qe[:, kcols]), s_all)
            ms = jnp.mean(o * o, axis=-1, keepdims=True)
            y = o * lax.rsqrt(ms + EPS) * ng * _silu(gate_ref[rows, vcols])
            o_ref[rows, vcols] = y.astype(BF16)


def _gla(u, w2p, b2, ng, B, S):
    R = GLA_ROWS
    nb = S // R
    row = lambda c: (lambda b, i: (b * nb + i, c))
    const = lambda b, i: (0, 0)
    return pl.pallas_call(
        _gla_kernel,
        grid=(B, nb),
        in_specs=[
            pl.BlockSpec((R, MIX_W), row(COL_AQK // MIX_W)),
            pl.BlockSpec((R, MIX_W), row(COL_AV // MIX_W)),
            pl.BlockSpec((R, MIX_W), row(COL_GA // MIX_W)),
            pl.BlockSpec((R, LANE), row(COL_SMALL // LANE)),
            pl.BlockSpec((LANE, GLA_H * GLA_DK), const),
            pl.BlockSpec((1, GLA_H * GLA_DK), const),
            pl.BlockSpec((1, GLA_DV), const),
        ],
        out_specs=pl.BlockSpec((R, MIX_W), row(0)),
        out_shape=jax.ShapeDtypeStruct((B * S, GLA_H * GLA_DV), BF16),
        scratch_shapes=[pltpu.VMEM((GLA_H, GLA_DV, GLA_DK), F32)],
        compiler_params=_cparams(("parallel", "arbitrary")),
        name="gla",
    )(u, u, u, u, w2p, b2, ng)


def _ssd_kernel(xbc_ref, z_ref, sm_ref, cw_ref, cb_ref, dtb_ref, a_ref, d_ref, ng_ref, o_ref,
                xpad_ref, st_ref, y_ref):
    R, L = SSD_ROWS, SSD_CHUNK
    HALO = SUBLANE

    @pl.when(pl.program_id(1) == 0)
    def _():
        st_ref[...] = jnp.zeros_like(st_ref)
        xpad_ref[...] = jnp.zeros((HALO, SSD_CONV_DIM), F32)

    x = xbc_ref[...]
    halo = xpad_ref[...]
    row = lax.broadcasted_iota(jnp.int32, (HALO, 1), 0)
    conv = cb_ref[...] + x * cw_ref[SSD_CONV - 1:SSD_CONV, :]
    for k in range(1, SSD_CONV):
        shifted = pltpu.roll(x, k, 0)
        head = jnp.where(row < k, pltpu.roll(halo, k, 0), shifted[0:HALO])
        conv = conv + jnp.concatenate([head, shifted[HALO:]], axis=0) * cw_ref[SSD_CONV - 1 - k:SSD_CONV - k, :]
    xpad_ref[...] = x[R - HALO:R]
    xact = _silu(conv)

    lane = lax.broadcasted_iota(jnp.int32, (1, LANE), 1)
    is_dt = (lane >= SM_DT) & (lane < SM_DT + SSD_H)
    pre = sm_ref[...] + dtb_ref[...]
    dt_all = jnp.maximum(pre, 0.0) + jnp.log(1.0 + jnp.exp(-jnp.abs(pre)))
    a_all = jnp.where(is_dt, dt_all * -jnp.exp(a_ref[...]), 0.0)
    ri = lax.broadcasted_iota(jnp.int32, (L, L), 0)
    ci = lax.broadcasted_iota(jnp.int32, (L, L), 1)
    lower = ci <= ri
    cum = jnp.where(lower, 1.0, 0.0).astype(BF16)

    heads_per_group = SSD_H // SSD_G
    for sb in range(R // L):
        rows = slice(sb * L, (sb + 1) * L)
        cs_all = _split_dot(cum, a_all[rows])
        cs_t = cs_all.T
        for g in range(SSD_G):
            bm = xact[rows, SSD_DINNER + g * SSD_N:SSD_DINNER + (g + 1) * SSD_N].astype(BF16)
            cm = xact[rows, SSD_DINNER + SSD_G * SSD_N + g * SSD_N:
                      SSD_DINNER + SSD_G * SSD_N + (g + 1) * SSD_N].astype(BF16)
            scores = _dot_nt(cm, bm)
            for hh in range(heads_per_group):
                h = g * heads_per_group + hh
                hcols = slice(h * SSD_HD, (h + 1) * SSD_HD)
                xs = xact[rows, hcols]
                dt = dt_all[rows, SM_DT + h:SM_DT + h + 1]
                cs_col = cs_all[:, SM_DT + h:SM_DT + h + 1]
                cs_row = cs_t[SM_DT + h:SM_DT + h + 1, :]
                cs_last = cs_col[L - 1:L, :]
                decay = jnp.where(lower, jnp.exp(cs_col - cs_row), 0.0)
                xdt = xs * dt
                y = _dot((scores * decay).astype(BF16), xdt.astype(BF16))
                prev = st_ref[h]
                y = y + _dot_nt(cm, prev.astype(BF16)) * jnp.exp(cs_col)
                st_ref[h] = prev * jnp.exp(cs_last) + _dot_tn(
                    (xdt * jnp.exp(cs_last - cs_col)).astype(BF16), bm)
                y_ref[rows, hcols] = y + xs * d_ref[:, hcols]

    yz = y_ref[...] * _silu(z_ref[...])
    gw = SSD_DINNER // SSD_G
    for g in range(SSD_G):
        yg = yz[:, g * gw:(g + 1) * gw]
        ms = jnp.mean(yg * yg, axis=-1, keepdims=True)
        o_ref[:, g * gw:(g + 1) * gw] = (yg * lax.rsqrt(ms + EPS) * ng_ref[:, g * gw:(g + 1) * gw]).astype(BF16)


def _ssd(u, cw, cb, dtb, a_pad, d_exp, ng, B, S):
    L = SSD_ROWS
    nb = S // L
    row = lambda c: (lambda b, i: (b * nb + i, c))
    const = lambda b, i: (0, 0)
    return pl.pallas_call(
        _ssd_kernel,
        grid=(B, nb),
        in_specs=[
            pl.BlockSpec((L, SSD_CONV_DIM), row(COL_XBC // SSD_CONV_DIM)),
            pl.BlockSpec((L, MIX_W), row(COL_GB // MIX_W)),
            pl.BlockSpec((L, LANE), row(COL_SMALL // LANE)),
            pl.BlockSpec((SSD_CONV, SSD_CONV_DIM), const),
            pl.BlockSpec((1, SSD_CONV_DIM), const),
            pl.BlockSpec((1, LANE), const),
            pl.BlockSpec((1, LANE), const),
            pl.BlockSpec((1, SSD_DINNER), const),
            pl.BlockSpec((1, SSD_DINNER), const),
        ],
        out_specs=pl.BlockSpec((L, SSD_DINNER), row(0)),
        out_shape=jax.ShapeDtypeStruct((B * S, SSD_DINNER), BF16),
        scratch_shapes=[pltpu.VMEM((SUBLANE, SSD_CONV_DIM), F32),
                        pltpu.VMEM((SSD_H, SSD_HD, SSD_N), F32),
                        pltpu.VMEM((L, SSD_DINNER), F32)],
        compiler_params=_cparams(("parallel", "arbitrary")),
        name="ssd",
    )(u, u, u, cw, cb, dtb, a_pad, d_exp, ng)


def _rope64(x, cm, s1, s2):
    return x * cm + pltpu.roll(x, LANE - MLA_ROPE // 2, 1) * s1 + pltpu.roll(x, MLA_ROPE // 2, 1) * s2


def _mla_prep_kernel(c_ref, sm_ref, cm_ref, s1_ref, s2_ref, wq_ref, wkv_ref, qng_ref, kvng_ref,
                     qhg_ref, khg_ref, q_ref, k_ref, v_ref):
    blk = c_ref[...]
    cq = blk[:, :MLA_Q_LORA]
    ckv = blk[:, MLA_Q_LORA:]
    def lane_tile_sum(x):
        tiles = [x[:, c:c + LANE] for c in range(0, x.shape[1], LANE)]
        return jnp.sum(functools.reduce(lambda a, b: a + b, tiles), axis=-1, keepdims=True)

    cqn = cq * lax.rsqrt(lane_tile_sum(cq * cq) * (1.0 / MLA_Q_LORA) + EPS) * qng_ref[...]
    ckvn = ckv * lax.rsqrt(lane_tile_sum(ckv * ckv) * (1.0 / MLA_KV_LORA) + EPS) * kvng_ref[...]
    qf = _dot(cqn.astype(BF16), wq_ref[...])
    kvf = _dot(ckvn.astype(BF16), wkv_ref[...])
    cm, s1, s2 = cm_ref[...], s1_ref[...], s2_ref[...]
    lane = lax.broadcasted_iota(jnp.int32, (1, LANE), 1)
    kr = jnp.where(lane < MLA_ROPE, sm_ref[...], 0.0)
    kr_sq = kr * kr
    qhg, khg = qhg_ref[...], khg_ref[...]
    kr_rot = _rope64(kr * khg[:, LANE:], cm, s1, s2)
    for h in range(MLA_H):
        qh = qf[:, MLA_PITCH * h:MLA_PITCH * (h + 1)]
        rq = lax.rsqrt(lane_tile_sum(qh * qh) * (1.0 / MLA_QK) + EPS)
        rq = rq * (MLA_QK ** -0.5 * LOG2E)
        q_ref[0, h, :, 0:LANE] = (qh[:, :LANE] * rq * qhg[:, :LANE]).astype(BF16)
        q_ref[0, h, :, LANE:2 * LANE] = _rope64(qh[:, LANE:] * rq * qhg[:, LANE:], cm, s1, s2).astype(BF16)
        kn = kvf[:, MLA_PITCH * h:MLA_PITCH * h + LANE]
        rk = lax.rsqrt(jnp.sum(kn * kn + kr_sq, axis=-1, keepdims=True) * (1.0 / MLA_QK) + EPS)
        k_ref[0, h, :, 0:LANE] = (kn * rk * khg[:, :LANE]).astype(BF16)
        k_ref[0, h, :, LANE:2 * LANE] = (kr_rot * rk).astype(BF16)
        v_ref[0, h, :, 0:MLA_V] = kvf[:, MLA_PITCH * h + LANE:MLA_PITCH * (h + 1)].astype(BF16)
        v_ref[0, h, :, MLA_V:2 * MLA_V] = jnp.ones((kvf.shape[0], MLA_V), BF16)


def _mla_prep(u, cm, s1, s2, wq, wkv, qng, kvng, qhg, khg, B, S):
    ts = min(PREP_ROWS, S)
    nb = S // ts
    row = lambda c: (lambda b, i: (b * nb + i, c))
    const = lambda b, i: (0, 0)
    hspec = lambda w: pl.BlockSpec((1, MLA_H, ts, w), lambda b, i: (b, 0, i, 0))
    return pl.pallas_call(
        _mla_prep_kernel,
        grid=(B, nb),
        in_specs=[
            pl.BlockSpec((ts, MIX_W), row(COL_CQKV // MIX_W)),
            pl.BlockSpec((ts, LANE), row(COL_SMALL // LANE)),
            pl.BlockSpec((ts, LANE), row(0)), pl.BlockSpec((ts, LANE), row(0)), pl.BlockSpec((ts, LANE), row(0)),
            pl.BlockSpec((MLA_Q_LORA, MLA_H * MLA_PITCH), const),
            pl.BlockSpec((MLA_KV_LORA, MLA_H * MLA_PITCH), const),
            pl.BlockSpec((1, MLA_Q_LORA), const),
            pl.BlockSpec((1, MLA_KV_LORA), const),
            pl.BlockSpec((1, MLA_PITCH), const),
            pl.BlockSpec((1, MLA_PITCH), const),
        ],
        out_specs=[hspec(MLA_PITCH), hspec(MLA_PITCH), hspec(2 * MLA_V)],
        out_shape=[jax.ShapeDtypeStruct((B, MLA_H, S, MLA_PITCH), BF16),
                   jax.ShapeDtypeStruct((B, MLA_H, S, MLA_PITCH), BF16),
                   jax.ShapeDtypeStruct((B, MLA_H, S, 2 * MLA_V), BF16)],
        compiler_params=_cparams(("parallel", "parallel")),
        name="mla_prep",
    )(u, u, cm, s1, s2, wq, wkv, qng, kvng, qhg, khg)


def _flash_kernel(q_ref, k_ref, v_ref, g_ref, o_ref, m_ref, acc_ref, sa_ref, sb_ref, *, t):
    i = pl.program_id(2)
    m_ref[...] = jnp.full_like(m_ref, -jnp.inf)
    acc_ref[...] = jnp.zeros_like(acc_ref)
    q = q_ref[0, 0]

    def rows(j):
        return pl.ds(pl.multiple_of(j * t, t), t)

    def scores(j):
        return _dot_nt(q, k_ref[0, 0, rows(j), :])

    def update(s, j, qrows=slice(0, t)):
        nk = s.shape[1]
        m_old = m_ref[qrows]
        m_new = jnp.maximum(m_old, jnp.max(s, axis=-1, keepdims=True))
        alpha = jnp.exp2(m_old - m_new)
        p = jnp.exp2(s - jnp.concatenate([m_new] * (nk // LANE), axis=1))
        vrows = pl.ds(pl.multiple_of(j * t, t), nk)
        acc_ref[qrows] = (jnp.concatenate([alpha, alpha], axis=1) * acc_ref[qrows]
                          + _dot(p.astype(BF16), v_ref[0, 0, vrows, :]))
        m_ref[qrows] = m_new

    def diagonal(s_ref):
        h = t // 2
        top = lax.broadcasted_iota(jnp.int32, (h, h), 1) <= lax.broadcasted_iota(jnp.int32, (h, h), 0)
        update(jnp.where(top, s_ref[0:h, 0:h], -jnp.inf), i, slice(0, h))
        low = lax.broadcasted_iota(jnp.int32, (h, t), 1) <= lax.broadcasted_iota(jnp.int32, (h, t), 0) + h
        update(jnp.where(low, s_ref[h:t, :], -jnp.inf), i, slice(h, t))

    sa_ref[...] = scores(0)

    def pair(p, carry):
        j = 2 * p
        s = sa_ref[...]
        sb_ref[...] = scores(j + 1)
        update(s, j)
        s = sb_ref[...]
        sa_ref[...] = scores(j + 2)
        update(s, j + 1)
        return carry

    lax.fori_loop(0, i // 2, pair, 0)

    @pl.when(i % 2 == 1)
    def _():
        s = sa_ref[...]
        sb_ref[...] = scores(i)
        update(s, i - 1)
        diagonal(sb_ref)

    @pl.when(i % 2 == 0)
    def _():
        diagonal(sa_ref)

    o_ref[...] = (acc_ref[:, :MLA_V] / acc_ref[:, MLA_V:] * _silu(g_ref[...])).astype(BF16)


def _flash(q, k, v, u, B, S):
    t = min(FLASH_TILE, S)
    nq = S // t
    return pl.pallas_call(
        functools.partial(_flash_kernel, t=t),
        grid=(B, MLA_H, nq),
        in_specs=[
            pl.BlockSpec((1, 1, t, MLA_PITCH), lambda b, h, i: (b, h, i, 0)),
            pl.BlockSpec((1, 1, S, MLA_PITCH), lambda b, h, i: (b, h, 0, 0)),
            pl.BlockSpec((1, 1, S, 2 * MLA_V), lambda b, h, i: (b, h, 0, 0)),
            pl.BlockSpec((t, LANE), lambda b, h, i: (b * nq + i, COL_GC // LANE + h)),
        ],
        out_specs=pl.BlockSpec((t, LANE), lambda b, h, i: (b * nq + i, h)),
        out_shape=jax.ShapeDtypeStruct((B * S, MLA_H * MLA_V), BF16),
        scratch_shapes=[pltpu.VMEM((t, LANE), F32), pltpu.VMEM((t, 2 * MLA_V), F32),
                        pltpu.VMEM((t, t), F32), pltpu.VMEM((t, t), F32)],
        compiler_params=_cparams(("parallel", "parallel", "arbitrary")),
        name="mla_flash",
    )(q, k, v, u)


def _dil_kernel(q_ref, k_ref, v_ref, g_ref, cd_ref, sd_ref, qg_ref, kg_ref, o_ref,
                qs_ref, ks_ref, qd_ref, kd_ref, vd_ref, oc_ref, ec_ref, tmp_ref, *, S):
    Q = DIL_BLK
    cd, sd = cd_ref[...], sd_ref[...]

    def norm_rope(x, g):
        xn = x * lax.rsqrt(jnp.mean(x * x, axis=-1, keepdims=True) + EPS) * g
        return xn * cd + pltpu.roll(xn, DIL_HD // 2, 1) * sd

    qs_ref[...] = norm_rope(q_ref[...], qg_ref[...] * (DIL_HD ** -0.5 * LOG2E))
    ks_ref[...] = norm_rope(k_ref[...], kg_ref[...])

    ra = lax.broadcasted_iota(jnp.int32, (Q, 2 * Q), 0)
    cc = lax.broadcasted_iota(jnp.int32, (Q, 2 * Q), 1)
    band = (cc >= ra) & (cc <= ra + Q)
    bias_inner = jnp.where(band, 0.0, -jnp.inf)
    bias_first = jnp.where(band & (cc >= Q), 0.0, -jnp.inf)
    n_units = S // Q
    vd_vals = vd_ref.at[:, 0:DIL_HD]
    vd_ref[:, DIL_HD:] = jnp.ones((vd_ref.shape[0], DIL_HD), BF16)

    for c, (w, d) in enumerate(DIL_CONFIGS):
        assert w // d == Q
        n_sub = S // d
        pitch = n_sub + Q
        nblk = n_sub // Q
        piece = min(n_sub, DIL_PIECE)
        streams = ((qs_ref, qd_ref, n_sub, 0), (ks_ref, kd_ref, pitch, Q), (v_ref, vd_vals, pitch, Q))
        for r in range(d):
            kd_ref[r * pitch:r * pitch + Q, :] = jnp.zeros((Q, DIL_HD), BF16)
            vd_vals[r * pitch:r * pitch + Q, :] = jnp.zeros((Q, DIL_HD), BF16)
        if d <= DIL_DIRECT_STRIDE:
            for r in range(d):
                for c0 in range(0, n_sub, piece):
                    src = pl.ds(r + c0 * d, piece, stride=d)
                    for src_ref, dst_ref, cpitch, lead in streams:
                        dst = r * cpitch + lead + c0
                        dst_ref[dst:dst + piece, :] = src_ref[src, :].astype(BF16)
        else:
            outer, inner = DIL_DIRECT_STRIDE, d // DIL_DIRECT_STRIDE
            n_outer = S // outer
            assert d % outer == 0 and inner <= DIL_DIRECT_STRIDE
            for r0 in range(outer):
                for src_ref, dst_ref, cpitch, lead in streams:
                    step = min(n_outer, DIL_PIECE)
                    for c0 in range(0, n_outer, step):
                        tmp_ref[c0:c0 + step, :] = src_ref[pl.ds(r0 + c0 * outer, step, stride=outer), :]
                    for m in range(inner):
                        dst = (r0 + outer * m) * cpitch + lead
                        dst_ref[dst:dst + n_sub, :] = tmp_ref[pl.ds(m, n_sub, stride=inner), :].astype(BF16)

        unroll = min(DIL_UNROLL, n_units)

        def group(n0, carry, c=c, n_sub=n_sub, pitch=pitch, nblk=nblk, unroll=unroll):
            for uu in range(unroll):
                n = n0 * unroll + uu
                r = n // nblk
                i = n - r * nblk
                qrows = pl.ds(pl.multiple_of(r * n_sub + i * Q, Q), Q)
                krows = pl.ds(pl.multiple_of(r * pitch + i * Q, Q), 2 * Q)
                s = _dot_nt(qd_ref[qrows, :], kd_ref[krows, :]) + jnp.where(i > 0, bias_inner, bias_first)
                m = jnp.max(s, axis=-1, keepdims=True)
                p = jnp.exp2(s - m)
                ol = _dot(p.astype(BF16), vd_ref[krows, :])
                l = ol[:, DIL_HD:]
                nat = pl.ds(i * (d * Q) + r, Q, stride=d) if d > 1 else pl.ds(pl.multiple_of(i * Q, Q), Q)
                oc_ref[c, nat, :] = ol[:, :DIL_HD] * (1.0 / l)
                ec_ref[c, nat, :] = m + jnp.log2(l)
            return carry

        lax.fori_loop(0, n_units // unroll, group, 0)

    rows_per_step = min(S, DIL_MERGE_ROWS)

    def merge(n, carry):
        rows = pl.ds(pl.multiple_of(n * rows_per_step, rows_per_step), rows_per_step)
        es = [ec_ref[c, rows, :] for c in range(len(DIL_CONFIGS))]
        e_max = functools.reduce(jnp.maximum, es)
        ws = [jnp.exp2(e - e_max) for e in es]
        num = sum(wt * oc_ref[c, rows, :] for c, wt in enumerate(ws))
        o_ref[rows, :] = (num / sum(ws) * _silu(g_ref[rows, :])).astype(BF16)
        return carry

    lax.fori_loop(0, S // rows_per_step, merge, 0)


def _dil(u, cd, sd, qg, kg, B, S):
    col = lambda c: (lambda b, h: (b, c // LANE + h))
    const = lambda b, h: (0, 0)
    blk = lambda im: pl.BlockSpec((S, LANE), im)
    kv_rows = max(S + d * DIL_BLK for _, d in DIL_CONFIGS)
    table = pl.BlockSpec((S, LANE), lambda b, h: (b, 0), pipeline_mode=pl.Buffered(1))
    return pl.pallas_call(
        functools.partial(_dil_kernel, S=S),
        grid=(B, DIL_H),
        in_specs=[blk(col(COL_DQ)), blk(col(COL_DK)), blk(col(COL_DV)), blk(col(COL_GD)),
                  table, table,
                  pl.BlockSpec((1, DIL_HD), const), pl.BlockSpec((1, DIL_HD), const)],
        out_specs=blk(lambda b, h: (b, h)),
        out_shape=jax.ShapeDtypeStruct((B * S, DIL_H * DIL_HD), BF16),
        scratch_shapes=[pltpu.VMEM((S, DIL_HD), F32), pltpu.VMEM((S, DIL_HD), F32),
                        pltpu.VMEM((S, DIL_HD), BF16),
                        pltpu.VMEM((kv_rows, DIL_HD), BF16), pltpu.VMEM((kv_rows, 2 * DIL_HD), BF16),
                        pltpu.VMEM((len(DIL_CONFIGS), S, DIL_HD), F32),
                        pltpu.VMEM((len(DIL_CONFIGS), S, LANE), F32),
                        pltpu.VMEM((S // DIL_DIRECT_STRIDE, DIL_HD), F32)],
        compiler_params=_cparams(("parallel", "parallel")),
        name="dilated",
    )(u, u, u, u, cd, sd, qg, kg)


SRC_ALR, SRC_XBC, SRC_DT, SRC_CQKV, SRC_KR, SRC_DQKV = 3072, 3088, 4112, 4120, 4632, 4696
N_IN = 6232
PACK_RUNS = ((0, COL_XBC, 0), (SRC_XBC, SSD_CONV_DIM, COL_XBC), (SRC_CQKV, MLA_Q_LORA + MLA_KV_LORA, COL_CQKV),
             (SRC_DQKV, 3 * MIX_W, COL_DQ))


def _pack_kernel(wt_ref, o_ref):
    tc = wt_ref.shape[2]
    for src, width, dst in PACK_RUNS:
        for c0 in range(0, width, PACK_PIECE):
            o_ref[0, :, dst + c0:dst + c0 + PACK_PIECE] = (
                wt_ref[0, src + c0:src + c0 + PACK_PIECE, :].T.astype(BF16))
    small = jnp.concatenate(
        [wt_ref[0, SRC_KR:SRC_KR + MLA_ROPE, :], wt_ref[0, SRC_ALR:SRC_ALR + GLA_GATE_RANK, :],
         wt_ref[0, SRC_DT:SRC_DT + SSD_H, :], jnp.zeros((LANE - SM_DT - SSD_H, tc), F32)], axis=0)
    o_ref[0, :, COL_SMALL:COL_END] = small.T.astype(BF16)
    o_ref[0, :, COL_END:N_PACK] = jnp.zeros((tc, N_PACK - COL_END), BF16)


def _pack_w_in(w):
    depth, kdim, n_in = w.shape
    assert n_in == N_IN and kdim % PACK_COLS == 0 and all(width % PACK_PIECE == 0 for _, width, _ in PACK_RUNS)
    return pl.pallas_call(
        _pack_kernel,
        grid=(depth, kdim // PACK_COLS),
        in_specs=[pl.BlockSpec((1, N_IN, PACK_COLS), lambda l, i: (l, 0, i))],
        out_specs=pl.BlockSpec((1, PACK_COLS, N_PACK), lambda l, i: (l, i, 0)),
        out_shape=jax.ShapeDtypeStruct((depth, kdim, N_PACK), BF16),
        compiler_params=_cparams(("parallel", "parallel")),
        name="pack_w_in",
    )(jnp.swapaxes(w, 1, 2))


def _row(v, width=None, offset=0):
    v = v.astype(F32).reshape(1, -1)
    if width is None:
        return v
    return jnp.pad(v, ((0, 0), (offset, width - offset - v.shape[1])))


def _pad_heads(w, n_heads, real, padded):
    k = w.shape[0]
    w = w.reshape(k, n_heads, real)
    return jnp.pad(w, ((0, 0), (0, 0), (0, padded - real))).reshape(k, n_heads * padded)


def kernel(x, positions, ln_g, w_in, w_out, gla_gate_w2, gla_gate_b, gla_norm_g, ssd_conv_w, ssd_conv_b,
           ssd_dt_bias, ssd_A_log, ssd_D, ssd_norm_g, mla_q_norm_g, mla_kv_norm_g, mla_w_uq, mla_w_ukv,
           mla_q_head_g, mla_k_head_g, dil_q_g, dil_k_g):
    B, S, D = x.shape
    depth = w_in.shape[0]
    assert D == D_MODEL and S % SSD_ROWS == 0 and S % GLA_ROWS == 0
    assert all(S % (d * DIL_BLK) == 0 for _, d in DIL_CONFIGS)
    T = B * S
    cm, s1, s2, cd, sd = _rope_tables(positions)
    w_in_packed = _pack_w_in(w_in)
    w_out_bf16 = w_out.astype(BF16)
    xf = x.reshape(T, D)
    for l in range(depth):
        u = _inproj(xf, _row(ln_g[l]), w_in_packed, l)
        w2p = jnp.pad(gla_gate_w2[l], ((SM_LR, LANE - SM_LR - GLA_GATE_RANK), (0, 0))).astype(BF16)
        ya = _gla(u, w2p, _row(gla_gate_b[l]), _row(gla_norm_g[l]), B, S)
        yb = _ssd(u, ssd_conv_w[l].astype(F32), _row(ssd_conv_b[l]),
                  _row(ssd_dt_bias[l], LANE, SM_DT), _row(ssd_A_log[l], LANE, SM_DT),
                  _row(jnp.repeat(ssd_D[l], SSD_HD)), _row(ssd_norm_g[l]), B, S)
        q, k, v = _mla_prep(u, cm, s1, s2,
                            _pad_heads(mla_w_uq[l], MLA_H, MLA_QK, MLA_PITCH).astype(BF16),
                            mla_w_ukv[l].astype(BF16), _row(mla_q_norm_g[l]), _row(mla_kv_norm_g[l]),
                            _row(mla_q_head_g[l], MLA_PITCH), _row(mla_k_head_g[l], MLA_PITCH), B, S)
        yc = _flash(q, k, v, u, B, S)
        yd = _dil(u, cd, sd, _row(dil_q_g[l]), _row(dil_k_g[l]), B, S)
        xf = _outproj(xf, ya, yb, yc, yd, w_out_bf16, l)
    return xf.reshape(B, S, D)
```

```python
import functools
import math

import jax
import jax.numpy as jnp
import numpy as np
from jax import lax
from jax.experimental import pallas as pl
from jax.experimental.pallas import tpu as pltpu

F32 = jnp.float32
BF16 = jnp.bfloat16

D_MODEL = 2048
GLA_H, GLA_DK, GLA_DV = 4, 64, 128
GLA_GATE_RANK = 16
GLA_GATE_TAU = 16.0
SSD_DINNER, SSD_HD, SSD_G, SSD_N, SSD_CONV = 512, 64, 2, 128, 4
SSD_H = SSD_DINNER // SSD_HD
SSD_CONV_DIM = SSD_DINNER + 2 * SSD_G * SSD_N
MLA_H, MLA_NOPE, MLA_ROPE, MLA_V = 4, 128, 64, 128
MLA_QK = MLA_NOPE + MLA_ROPE
MLA_Q_LORA, MLA_KV_LORA = 384, 128
DIL_H, DIL_HD = 4, 128
DIL_CONFIGS = ((128, 1), (512, 4), (2048, 16))
ROPE_THETA = 10000.0
EPS = 1e-6
LOG2E = math.log2(math.e)
D_MIX = GLA_H * GLA_DV + SSD_DINNER + MLA_H * MLA_V + DIL_H * DIL_HD

LANE = 128
SUBLANE = 8
MXU_DIM = 256
VMEM_REQUEST = 52 * 1024 * 1024
MIX_W = 512
MLA_PITCH = 2 * LANE

COL_GA, COL_GB, COL_GC, COL_GD = 0, 512, 1024, 1536
COL_AQK = 2048
COL_AV = 2560
COL_XBC = 3072
COL_CQKV = 4096
COL_DQ, COL_DK, COL_DV = 4608, 5120, 5632
COL_SMALL = 6144
COL_END = COL_SMALL + LANE
INPROJ_TN = 5 * MXU_DIM
N_PACK = -(-COL_END // INPROJ_TN) * INPROJ_TN
SM_KR, SM_LR, SM_DT = 0, 64, 80

GLA_CHUNK = 32
GLA_ROWS = 1024
GLA_BLOCK = 512
SSD_CHUNK = 256
SSD_ROWS = 512
DIL_BLK = 128
DIL_UNROLL = 16
DIL_DIRECT_STRIDE = 4
DIL_PIECE = 512
DIL_MERGE_ROWS = 256
INPROJ_TM = 1024
INPROJ_PAIR = 2
OUTPROJ_TM = 512
PREP_ROWS = 512
FLASH_TILE = 1024
PACK_COLS = 256
PACK_PIECE = 512


def _cparams(sem):
    return pltpu.CompilerParams(dimension_semantics=sem, vmem_limit_bytes=VMEM_REQUEST)


def _silu(x):
    return x * (1.0 / (1.0 + jnp.exp(-x)))


def _dot(a, b):
    return jnp.dot(a, b, preferred_element_type=F32)


def _dot_nt(a, b):
    return lax.dot_general(a, b, (((1,), (1,)), ((), ())), preferred_element_type=F32)


def _dot_tn(a, b):
    return lax.dot_general(a, b, (((0,), (0,)), ((), ())), preferred_element_type=F32)


def _split_dot(tri, x):
    hi = x.astype(BF16)
    lo = (x - hi.astype(F32)).astype(BF16)
    return _dot(tri, hi) + _dot(tri, lo)


def _rope_tables_kernel(pos_ref, f_ref, cm_ref, s1_ref, s2_ref, cd_ref, sd_ref):
    half = LANE // 2
    ang = pos_ref[...] * f_ref[...]
    cos, sin = jnp.cos(ang), jnp.sin(ang)
    lane = lax.broadcasted_iota(jnp.int32, (1, LANE), 1)
    low = lane < half
    cm_ref[...] = jnp.where(low, cos, 0.0)
    s1_ref[...] = jnp.where(lane < MLA_ROPE // 2, -sin, 0.0)
    s2_ref[...] = jnp.where((lane >= MLA_ROPE // 2) & low, sin, 0.0)
    cd_ref[...] = jnp.where(low, pltpu.roll(cos, half, 1), cos)
    sd_ref[...] = jnp.where(low, -pltpu.roll(sin, half, 1), sin)


def _rope_tables(positions):
    assert MLA_ROPE == LANE // 2 and DIL_HD == LANE
    T = positions.size
    ts = min(T, 2048)
    pos = positions.reshape(T, 1).astype(F32)
    lane = np.arange(LANE // 2)
    fm = np.exp(-math.log(ROPE_THETA) * (lane % (MLA_ROPE // 2)) * (2.0 / MLA_ROPE))
    fd = np.exp(-math.log(ROPE_THETA) * lane * (2.0 / DIL_HD))
    freqs = jnp.asarray(np.concatenate([fm, fd]), F32).reshape(1, LANE)
    row = pl.BlockSpec((ts, LANE), lambda i: (i, 0))
    return pl.pallas_call(
        _rope_tables_kernel,
        grid=(T // ts,),
        in_specs=[pl.BlockSpec((ts, 1), lambda i: (i, 0)), pl.BlockSpec((1, LANE), lambda i: (0, 0))],
        out_specs=[row] * 5,
        out_shape=[jax.ShapeDtypeStruct((T, LANE), F32)] * 5,
        compiler_params=_cparams(("parallel",)),
        name="rope_tables",
    )(pos, freqs)


def _inproj_kernel(x_ref, g_ref, w_ref, o_ref, h_ref):
    r = pl.program_id(2)

    @pl.when(pl.program_id(1) == 0)
    def _():
        x = x_ref[...]
        ms = jnp.mean(x * x, axis=-1, keepdims=True)
        h_ref[r] = (x * lax.rsqrt(ms + EPS) * g_ref[...]).astype(BF16)

    o_ref[...] = _dot(h_ref[r], w_ref[...])


def _inproj(x, g, w, layer):
    T = x.shape[0]
    tm, tn = min(INPROJ_TM, T), INPROJ_TN
    pair = INPROJ_PAIR if T % (INPROJ_PAIR * tm) == 0 else 1
    last = pair - 1
    return pl.pallas_call(
        _inproj_kernel,
        grid=(T // (pair * tm), N_PACK // tn, pair),
        in_specs=[
            pl.BlockSpec((tm, D_MODEL), lambda i, j, r: (pair * i + jnp.where(j == 0, r, last), 0)),
            pl.BlockSpec((1, D_MODEL), lambda i, j, r: (0, 0)),
            pl.BlockSpec((None, D_MODEL, tn), lambda i, j, r: (layer, 0, j)),
        ],
        out_specs=pl.BlockSpec((tm, tn), lambda i, j, r: (pair * i + r, j)),
        out_shape=jax.ShapeDtypeStruct((T, N_PACK), F32),
        scratch_shapes=[pltpu.VMEM((pair, tm, D_MODEL), BF16)],
        compiler_params=_cparams(("parallel", "arbitrary", "arbitrary")),
        name="inproj",
    )(x, g, w)


def _outproj_kernel(x_ref, ya_ref, yb_ref, yc_ref, yd_ref, w_ref, o_ref):
    acc = x_ref[...]
    for n, y_ref in enumerate((ya_ref, yb_ref, yc_ref, yd_ref)):
        acc = acc + _dot(y_ref[...], w_ref[n * MIX_W:(n + 1) * MIX_W, :])
    o_ref[...] = acc


def _outproj(x, ya, yb, yc, yd, w, layer):
    T = x.shape[0]
    tm = min(OUTPROJ_TM, T)
    yspec = pl.BlockSpec((tm, MIX_W), lambda i: (i, 0))
    return pl.pallas_call(
        _outproj_kernel,
        grid=(T // tm,),
        in_specs=[pl.BlockSpec((tm, D_MODEL), lambda i: (i, 0)), yspec, yspec, yspec, yspec,
                  pl.BlockSpec((None, D_MIX, D_MODEL), lambda i: (layer, 0, 0))],
        out_specs=pl.BlockSpec((tm, D_MODEL), lambda i: (i, 0)),
        out_shape=jax.ShapeDtypeStruct((T, D_MODEL), F32),
        compiler_params=_cparams(("parallel",)),
        name="outproj",
    )(x, ya, yb, yc, yd, w)


def _gla_kernel(qk_ref, v_ref, gate_ref, sm_ref, w2_ref, b2_ref, ng_ref, o_ref, st_ref):
    @pl.when(pl.program_id(1) == 0)
    def _():
        st_ref[...] = jnp.zeros_like(st_ref)

    R, RB, C = GLA_ROWS, GLA_BLOCK, GLA_CHUNK
    xg = _dot(sm_ref[...].astype(BF16), w2_ref[...]) + b2_ref[...]
    logd = (jnp.minimum(xg, 0.0) - jnp.log(1.0 + jnp.exp(-jnp.abs(xg)))) * (1.0 / GLA_GATE_TAU)
    ri = lax.broadcasted_iota(jnp.int32, (RB, RB), 0)
    ci = lax.broadcasted_iota(jnp.int32, (RB, RB), 1)
    back = ri - ci
    in_chunk = (back >= 0) & (back <= (ri & (C - 1)))
    cum = jnp.where(in_chunk, 1.0, 0.0).astype(BF16)
    ng = ng_ref[...]
    n_chunks = RB // C
    hk = GLA_H * GLA_DK
    group = MXU_DIM // GLA_DK
    wide, grows = group * GLA_DK, group * C
    n_groups = RB // grows
    place = ((lax.broadcasted_iota(jnp.int32, (grows, wide), 0) >> (C.bit_length() - 1))
             == (lax.broadcasted_iota(jnp.int32, (grows, wide), 1) >> (GLA_DK.bit_length() - 1)))

    def block_diag(x):
        pair = jnp.concatenate([x, x], axis=1)
        return jnp.where(place, jnp.concatenate([pair] * (group // 2), axis=1), jnp.zeros((), x.dtype))

    for sb in range(R // RB):
        rows = slice(sb * RB, (sb + 1) * RB)
        bc = _split_dot(cum, logd[rows]).reshape(n_chunks, C, hk)
        b_mid = bc[:, C // 2 - 1:C // 2, :]
        b_last = bc[:, C - 1:C, :]
        q = (qk_ref[rows, :hk] * (GLA_DK ** -0.5)).reshape(n_chunks, C, hk)
        k = qk_ref[rows, hk:].reshape(n_chunks, C, hk)
        qd = (q * jnp.exp(bc - b_mid)).astype(BF16).reshape(RB, hk)
        kd = (k * jnp.exp(b_mid - bc)).astype(BF16).reshape(RB, hk)
        qe = (q * jnp.exp(bc)).astype(BF16).reshape(RB, hk)
        kl = (k * jnp.exp(b_last - bc)).astype(BF16).reshape(RB, hk)
        e_last = jnp.exp(b_last)
        for h in range(GLA_H):
            kcols = slice(h * GLA_DK, (h + 1) * GLA_DK)
            vcols = slice(h * GLA_DV, (h + 1) * GLA_DV)
            v = v_ref[rows, vcols].astype(BF16)
            attn = jnp.where(in_chunk, _dot_nt(qd[:, kcols], kd[:, kcols]), 0.0).astype(BF16)
            u_grp = [_dot_tn(v[g * grows:(g + 1) * grows], block_diag(kl[g * grows:(g + 1) * grows, kcols]))
                     for g in range(n_groups)]
            st = st_ref[h]
            entering = []
            for c in range(n_chunks):
                entering.append(st)
                cg = c % group
                st = st * e_last[c][:, kcols] + u_grp[c // group][:, cg * GLA_DK:(cg + 1) * GLA_DK]
            st_ref[h] = st
            o_inter = [_dot_nt(block_diag(qe[g * grows:(g + 1) * grows, kcols]),
                               jnp.concatenate(entering[g * group:(g + 1) * group], axis=1).astype(BF16))
                       for g in range(n_groups)]
            o = _dot(attn, v) + jnp.concatenate(o_inter, axis=0)
            ms = jnp.mean(o * o, axis=-1, keepdims=True)
            y = o * lax.rsqrt(ms + EPS) * ng * _silu(gate_ref[rows, vcols])
            o_ref[rows, vcols] = y.astype(BF16)


def _gla(u, w2p, b2, ng, B, S):
    R = GLA_ROWS
    nb = S // R
    row = lambda c: (lambda b, i: (b * nb + i, c))
    const = lambda b, i: (0, 0)
    return pl.pallas_call(
        _gla_kernel,
        grid=(B, nb),
        in_specs=[
            pl.BlockSpec((R, MIX_W), row(COL_AQK // MIX_W)),
            pl.BlockSpec((R, MIX_W), row(COL_AV // MIX_W)),
            pl.BlockSpec((R, MIX_W), row(COL_GA // MIX_W)),
            pl.BlockSpec((R, LANE), row(COL_SMALL // LANE)),
            pl.BlockSpec((LANE, GLA_H * GLA_DK), const),
            pl.BlockSpec((1, GLA_H * GLA_DK), const),
            pl.BlockSpec((1, GLA_DV), const),
        ],
        out_specs=pl.BlockSpec((R, MIX_W), row(0)),
        out_shape=jax.ShapeDtypeStruct((B * S, GLA_H * GLA_DV), BF16),
        scratch_shapes=[pltpu.VMEM((GLA_H, GLA_DV, GLA_DK), F32)],
        compiler_params=_cparams(("parallel", "arbitrary")),
        name="gla",
    )(u, u, u, u, w2p, b2, ng)


def _ssd_kernel(xbc_ref, z_ref, sm_ref, cw_ref, cb_ref, dtb_ref, a_ref, d_ref, ng_ref, o_ref,
                xpad_ref, st_ref, y_ref):
    R, L = SSD_ROWS, SSD_CHUNK
    HALO = SUBLANE

    @pl.when(pl.program_id(1) == 0)
    def _():
        st_ref[...] = jnp.zeros_like(st_ref)
        xpad_ref[...] = jnp.zeros((HALO, SSD_CONV_DIM), F32)

    x = xbc_ref[...]
    halo = xpad_ref[...]
    row = lax.broadcasted_iota(jnp.int32, (HALO, 1), 0)
    conv = cb_ref[...] + x * cw_ref[SSD_CONV - 1:SSD_CONV, :]
    for k in range(1, SSD_CONV):
        shifted = pltpu.roll(x, k, 0)
        head = jnp.where(row < k, pltpu.roll(halo, k, 0), shifted[0:HALO])
        conv = conv + jnp.concatenate([head, shifted[HALO:]], axis=0) * cw_ref[SSD_CONV - 1 - k:SSD_CONV - k, :]
    xpad_ref[...] = x[R - HALO:R]
    xact = _silu(conv)

    lane = lax.broadcasted_iota(jnp.int32, (1, LANE), 1)
    is_dt = (lane >= SM_DT) & (lane < SM_DT + SSD_H)
    pre = sm_ref[...] + dtb_ref[...]
    dt_all = jnp.maximum(pre, 0.0) + jnp.log(1.0 + jnp.exp(-jnp.abs(pre)))
    a_all = jnp.where(is_dt, dt_all * -jnp.exp(a_ref[...]), 0.0)
    ri = lax.broadcasted_iota(jnp.int32, (L, L), 0)
    ci = lax.broadcasted_iota(jnp.int32, (L, L), 1)
    lower = ci <= ri
    cum = jnp.where(lower, 1.0, 0.0).astype(BF16)

    heads_per_group = SSD_H // SSD_G
    for sb in range(R // L):
        rows = slice(sb * L, (sb + 1) * L)
        cs_all = _split_dot(cum, a_all[rows])
        cs_t = cs_all.T
        for g in range(SSD_G):
            bm = xact[rows, SSD_DINNER + g * SSD_N:SSD_DINNER + (g + 1) * SSD_N].astype(BF16)
            cm = xact[rows, SSD_DINNER + SSD_G * SSD_N + g * SSD_N:
                      SSD_DINNER + SSD_G * SSD_N + (g + 1) * SSD_N].astype(BF16)
            scores = _dot_nt(cm, bm)
            for hh in range(heads_per_group):
                h = g * heads_per_group + hh
                hcols = slice(h * SSD_HD, (h + 1) * SSD_HD)
                xs = xact[rows, hcols]
                dt = dt_all[rows, SM_DT + h:SM_DT + h + 1]
                cs_col = cs_all[:, SM_DT + h:SM_DT + h + 1]
                cs_row = cs_t[SM_DT + h:SM_DT + h + 1, :]
                cs_last = cs_col[L - 1:L, :]
                decay = jnp.where(lower, jnp.exp(cs_col - cs_row), 0.0)
                xdt = xs * dt
                y = _dot((scores * decay).astype(BF16), xdt.astype(BF16))
                prev = st_ref[h]
                y = y + _dot_nt(cm, prev.astype(BF16)) * jnp.exp(cs_col)
                st_ref[h] = prev * jnp.exp(cs_last) + _dot_tn(
                    (xdt * jnp.exp(cs_last - cs_col)).astype(BF16), bm)
                y_ref[rows, hcols] = y + xs * d_ref[:, hcols]

    yz = y_ref[...] * _silu(z_ref[...])
    gw = SSD_DINNER // SSD_G
    for g in range(SSD_G):
        yg = yz[:, g * gw:(g + 1) * gw]
        ms = jnp.mean(yg * yg, axis=-1, keepdims=True)
        o_ref[:, g * gw:(g + 1) * gw] = (yg * lax.rsqrt(ms + EPS) * ng_ref[:, g * gw:(g + 1) * gw]).astype(BF16)


def _ssd(u, cw, cb, dtb, a_pad, d_exp, ng, B, S):
    L = SSD_ROWS
    nb = S // L
    row = lambda c: (lambda b, i: (b * nb + i, c))
    const = lambda b, i: (0, 0)
    return pl.pallas_call(
        _ssd_kernel,
        grid=(B, nb),
        in_specs=[
            pl.BlockSpec((L, SSD_CONV_DIM), row(COL_XBC // SSD_CONV_DIM)),
            pl.BlockSpec((L, MIX_W), row(COL_GB // MIX_W)),
            pl.BlockSpec((L, LANE), row(COL_SMALL // LANE)),
            pl.BlockSpec((SSD_CONV, SSD_CONV_DIM), const),
            pl.BlockSpec((1, SSD_CONV_DIM), const),
            pl.BlockSpec((1, LANE), const),
            pl.BlockSpec((1, LANE), const),
            pl.BlockSpec((1, SSD_DINNER), const),
            pl.BlockSpec((1, SSD_DINNER), const),
        ],
        out_specs=pl.BlockSpec((L, SSD_DINNER), row(0)),
        out_shape=jax.ShapeDtypeStruct((B * S, SSD_DINNER), BF16),
        scratch_shapes=[pltpu.VMEM((SUBLANE, SSD_CONV_DIM), F32),
                        pltpu.VMEM((SSD_H, SSD_HD, SSD_N), F32),
                        pltpu.VMEM((L, SSD_DINNER), F32)],
        compiler_params=_cparams(("parallel", "arbitrary")),
        name="ssd",
    )(u, u, u, cw, cb, dtb, a_pad, d_exp, ng)


def _rope64(x, cm, s1, s2):
    return x * cm + pltpu.roll(x, LANE - MLA_ROPE // 2, 1) * s1 + pltpu.roll(x, MLA_ROPE // 2, 1) * s2


def _mla_prep_kernel(c_ref, sm_ref, cm_ref, s1_ref, s2_ref, wq_ref, wkv_ref, qng_ref, kvng_ref,
                     qhg_ref, khg_ref, q_ref, k_ref, v_ref):
    blk = c_ref[...]
    cq = blk[:, :MLA_Q_LORA]
    ckv = blk[:, MLA_Q_LORA:]
    def lane_tile_sum(x):
        tiles = [x[:, c:c + LANE] for c in range(0, x.shape[1], LANE)]
        return jnp.sum(functools.reduce(lambda a, b: a + b, tiles), axis=-1, keepdims=True)

    cqn = cq * lax.rsqrt(lane_tile_sum(cq * cq) * (1.0 / MLA_Q_LORA) + EPS) * qng_ref[...]
    ckvn = ckv * lax.rsqrt(lane_tile_sum(ckv * ckv) * (1.0 / MLA_KV_LORA) + EPS) * kvng_ref[...]
    qf = _dot(cqn.astype(BF16), wq_ref[...])
    kvf = _dot(ckvn.astype(BF16), wkv_ref[...])
    cm, s1, s2 = cm_ref[...], s1_ref[...], s2_ref[...]
    lane = lax.broadcasted_iota(jnp.int32, (1, LANE), 1)
    kr = jnp.where(lane < MLA_ROPE, sm_ref[...], 0.0)
    kr_sq = kr * kr
    qhg, khg = qhg_ref[...], khg_ref[...]
    kr_rot = _rope64(kr * khg[:, LANE:], cm, s1, s2)
    for h in range(MLA_H):
        qh = qf[:, MLA_PITCH * h:MLA_PITCH * (h + 1)]
        rq = lax.rsqrt(lane_tile_sum(qh * qh) * (1.0 / MLA_QK) + EPS)
        rq = rq * (MLA_QK ** -0.5 * LOG2E)
        q_ref[0, h, :, 0:LANE] = (qh[:, :LANE] * rq * qhg[:, :LANE]).astype(BF16)
        q_ref[0, h, :, LANE:2 * LANE] = _rope64(qh[:, LANE:] * rq * qhg[:, LANE:], cm, s1, s2).astype(BF16)
        kn = kvf[:, MLA_PITCH * h:MLA_PITCH * h + LANE]
        rk = lax.rsqrt(jnp.sum(kn * kn + kr_sq, axis=-1, keepdims=True) * (1.0 / MLA_QK) + EPS)
        k_ref[0, h, :, 0:LANE] = (kn * rk * khg[:, :LANE]).astype(BF16)
        k_ref[0, h, :, LANE:2 * LANE] = (kr_rot * rk).astype(BF16)
        v_ref[0, h, :, 0:MLA_V] = kvf[:, MLA_PITCH * h + LANE:MLA_PITCH * (h + 1)].astype(BF16)
        v_ref[0, h, :, MLA_V:2 * MLA_V] = jnp.ones((kvf.shape[0], MLA_V), BF16)


def _mla_prep(u, cm, s1, s2, wq, wkv, qng, kvng, qhg, khg, B, S):
    ts = min(PREP_ROWS, S)
    nb = S // ts
    row = lambda c: (lambda b, i: (b * nb + i, c))
    const = lambda b, i: (0, 0)
    hspec = lambda w: pl.BlockSpec((1, MLA_H, ts, w), lambda b, i: (b, 0, i, 0))
    return pl.pallas_call(
        _mla_prep_kernel,
        grid=(B, nb),
        in_specs=[
            pl.BlockSpec((ts, MIX_W), row(COL_CQKV // MIX_W)),
            pl.BlockSpec((ts, LANE), row(COL_SMALL // LANE)),
            pl.BlockSpec((ts, LANE), row(0)), pl.BlockSpec((ts, LANE), row(0)), pl.BlockSpec((ts, LANE), row(0)),
            pl.BlockSpec((MLA_Q_LORA, MLA_H * MLA_PITCH), const),
            pl.BlockSpec((MLA_KV_LORA, MLA_H * MLA_PITCH), const),
            pl.BlockSpec((1, MLA_Q_LORA), const),
            pl.BlockSpec((1, MLA_KV_LORA), const),
            pl.BlockSpec((1, MLA_PITCH), const),
            pl.BlockSpec((1, MLA_PITCH), const),
        ],
        out_specs=[hspec(MLA_PITCH), hspec(MLA_PITCH), hspec(2 * MLA_V)],
        out_shape=[jax.ShapeDtypeStruct((B, MLA_H, S, MLA_PITCH), BF16),
                   jax.ShapeDtypeStruct((B, MLA_H, S, MLA_PITCH), BF16),
                   jax.ShapeDtypeStruct((B, MLA_H, S, 2 * MLA_V), BF16)],
        compiler_params=_cparams(("parallel", "parallel")),
        name="mla_prep",
    )(u, u, cm, s1, s2, wq, wkv, qng, kvng, qhg, khg)


def _flash_kernel(q_ref, k_ref, v_ref, g_ref, o_ref, m_ref, acc_ref, sa_ref, sb_ref, *, t):
    i = pl.program_id(2)
    m_ref[...] = jnp.full_like(m_ref, -jnp.inf)
    acc_ref[...] = jnp.zeros_like(acc_ref)
    q = q_ref[0, 0]

    def rows(j):
        return pl.ds(pl.multiple_of(j * t, t), t)

    def scores(j):
        return _dot_nt(q, k_ref[0, 0, rows(j), :])

    def update(s, j, qrows=slice(0, t)):
        nk = s.shape[1]
        m_old = m_ref[qrows]
        m_new = jnp.maximum(m_old, jnp.max(s, axis=-1, keepdims=True))
        alpha = jnp.exp2(m_old - m_new)
        p = jnp.exp2(s - jnp.concatenate([m_new] * (nk // LANE), axis=1))
        vrows = pl.ds(pl.multiple_of(j * t, t), nk)
        acc_ref[qrows] = (jnp.concatenate([alpha, alpha], axis=1) * acc_ref[qrows]
                          + _dot(p.astype(BF16), v_ref[0, 0, vrows, :]))
        m_ref[qrows] = m_new

    def diagonal(s_ref):
        h = t // 2
        top = lax.broadcasted_iota(jnp.int32, (h, h), 1) <= lax.broadcasted_iota(jnp.int32, (h, h), 0)
        update(jnp.where(top, s_ref[0:h, 0:h], -jnp.inf), i, slice(0, h))
        low = lax.broadcasted_iota(jnp.int32, (h, t), 1) <= lax.broadcasted_iota(jnp.int32, (h, t), 0) + h
        update(jnp.where(low, s_ref[h:t, :], -jnp.inf), i, slice(h, t))

    sa_ref[...] = scores(0)

    def pair(p, carry):
        j = 2 * p
        s = sa_ref[...]
        sb_ref[...] = scores(j + 1)
        update(s, j)
        s = sb_ref[...]
        sa_ref[...] = scores(j + 2)
        update(s, j + 1)
        return carry

    lax.fori_loop(0, i // 2, pair, 0)

    @pl.when(i % 2 == 1)
    def _():
        s = sa_ref[...]
        sb_ref[...] = scores(i)
        update(s, i - 1)
        diagonal(sb_ref)

    @pl.when(i % 2 == 0)
    def _():
        diagonal(sa_ref)

    o_ref[...] = (acc_ref[:, :MLA_V] / acc_ref[:, MLA_V:] * _silu(g_ref[...])).astype(BF16)


def _flash(q, k, v, u, B, S):
    t = min(FLASH_TILE, S)
    nq = S // t
    return pl.pallas_call(
        functools.partial(_flash_kernel, t=t),
        grid=(B, MLA_H, nq),
        in_specs=[
            pl.BlockSpec((1, 1, t, MLA_PITCH), lambda b, h, i: (b, h, i, 0)),
            pl.BlockSpec((1, 1, S, MLA_PITCH), lambda b, h, i: (b, h, 0, 0)),
            pl.BlockSpec((1, 1, S, 2 * MLA_V), lambda b, h, i: (b, h, 0, 0)),
            pl.BlockSpec((t, LANE), lambda b, h, i: (b * nq + i, COL_GC // LANE + h)),
        ],
        out_specs=pl.BlockSpec((t, LANE), lambda b, h, i: (b * nq + i, h)),
        out_shape=jax.ShapeDtypeStruct((B * S, MLA_H * MLA_V), BF16),
        scratch_shapes=[pltpu.VMEM((t, LANE), F32), pltpu.VMEM((t, 2 * MLA_V), F32),
                        pltpu.VMEM((t, t), F32), pltpu.VMEM((t, t), F32)],
        compiler_params=_cparams(("parallel", "parallel", "arbitrary")),
        name="mla_flash",
    )(q, k, v, u)


def _dil_kernel(q_ref, k_ref, v_ref, g_ref, cd_ref, sd_ref, qg_ref, kg_ref, o_ref,
                qs_ref, ks_ref, qd_ref, kd_ref, vd_ref, oc_ref, ec_ref, tmp_ref, *, S):
    Q = DIL_BLK
    cd, sd = cd_ref[...], sd_ref[...]

    def norm_rope(x, g):
        xn = x * lax.rsqrt(jnp.mean(x * x, axis=-1, keepdims=True) + EPS) * g
        return xn * cd + pltpu.roll(xn, DIL_HD // 2, 1) * sd

    qs_ref[...] = norm_rope(q_ref[...], qg_ref[...] * (DIL_HD ** -0.5 * LOG2E))
    ks_ref[...] = norm_rope(k_ref[...], kg_ref[...])

    ra = lax.broadcasted_iota(jnp.int32, (Q, 2 * Q), 0)
    cc = lax.broadcasted_iota(jnp.int32, (Q, 2 * Q), 1)
    band = (cc >= ra) & (cc <= ra + Q)
    bias_inner = jnp.where(band, 0.0, -jnp.inf)
    bias_first = jnp.where(band & (cc >= Q), 0.0, -jnp.inf)
    n_units = S // Q
    vd_vals = vd_ref.at[:, 0:DIL_HD]
    vd_ref[:, DIL_HD:] = jnp.ones((vd_ref.shape[0], DIL_HD), BF16)

    for c, (w, d) in enumerate(DIL_CONFIGS):
        assert w // d == Q
        n_sub = S // d
        pitch = n_sub + Q
        nblk = n_sub // Q
        piece = min(n_sub, DIL_PIECE)
        streams = ((qs_ref, qd_ref, n_sub, 0), (ks_ref, kd_ref, pitch, Q), (v_ref, vd_vals, pitch, Q))
        for r in range(d):
            kd_ref[r * pitch:r * pitch + Q, :] = jnp.zeros((Q, DIL_HD), BF16)
            vd_vals[r * pitch:r * pitch + Q, :] = jnp.zeros((Q, DIL_HD), BF16)
        if d <= DIL_DIRECT_STRIDE:
            for r in range(d):
                for c0 in range(0, n_sub, piece):
                    src = pl.ds(r + c0 * d, piece, stride=d)
                    for src_ref, dst_ref, cpitch, lead in streams:
                        dst = r * cpitch + lead + c0
                        dst_ref[dst:dst + piece, :] = src_ref[src, :].astype(BF16)
        else:
            outer, inner = DIL_DIRECT_STRIDE, d // DIL_DIRECT_STRIDE
            n_outer = S // outer
            assert d % outer == 0 and inner <= DIL_DIRECT_STRIDE
            for r0 in range(outer):
                for src_ref, dst_ref, cpitch, lead in streams:
                    step = min(n_outer, DIL_PIECE)
                    for c0 in range(0, n_outer, step):
                        tmp_ref[c0:c0 + step, :] = src_ref[pl.ds(r0 + c0 * outer, step, stride=outer), :]
                    for m in range(inner):
                        dst = (r0 + outer * m) * cpitch + lead
                        dst_ref[dst:dst + n_sub, :] = tmp_ref[pl.ds(m, n_sub, stride=inner), :].astype(BF16)

        unroll = min(DIL_UNROLL, n_units)

        def group(n0, carry, c=c, n_sub=n_sub, pitch=pitch, nblk=nblk, unroll=unroll):
            for uu in range(unroll):
                n = n0 * unroll + uu
                r = n // nblk
                i = n - r * nblk
                qrows = pl.ds(pl.multiple_of(r * n_sub + i * Q, Q), Q)
                krows = pl.ds(pl.multiple_of(r * pitch + i * Q, Q), 2 * Q)
                s = _dot_nt(qd_ref[qrows, :], kd_ref[krows, :]) + jnp.where(i > 0, bias_inner, bias_first)
                m = jnp.max(s, axis=-1, keepdims=True)
                p = jnp.exp2(s - m)
                ol = _dot(p.astype(BF16), vd_ref[krows, :])
                l = ol[:, DIL_HD:]
                nat = pl.ds(i * (d * Q) + r, Q, stride=d) if d > 1 else pl.ds(pl.multiple_of(i * Q, Q), Q)
                oc_ref[c, nat, :] = ol[:, :DIL_HD] * (1.0 / l)
                ec_ref[c, nat, :] = m + jnp.log2(l)
            return carry

        lax.fori_loop(0, n_units // unroll, group, 0)

    rows_per_step = min(S, DIL_MERGE_ROWS)

    def merge(n, carry):
        rows = pl.ds(pl.multiple_of(n * rows_per_step, rows_per_step), rows_per_step)
        es = [ec_ref[c, rows, :] for c in range(len(DIL_CONFIGS))]
        e_max = functools.reduce(jnp.maximum, es)
        ws = [jnp.exp2(e - e_max) for e in es]
        num = sum(wt * oc_ref[c, rows, :] for c, wt in enumerate(ws))
        o_ref[rows, :] = (num / sum(ws) * _silu(g_ref[rows, :])).astype(BF16)
        return carry

    lax.fori_loop(0, S // rows_per_step, merge, 0)


def _dil(u, cd, sd, qg, kg, B, S):
    col = lambda c: (lambda b, h: (b, c // LANE + h))
    const = lambda b, h: (0, 0)
    blk = lambda im: pl.BlockSpec((S, LANE), im)
    kv_rows = max(S + d * DIL_BLK for _, d in DIL_CONFIGS)
    table = pl.BlockSpec((S, LANE), lambda b, h: (b, 0), pipeline_mode=pl.Buffered(1))
    return pl.pallas_call(
        functools.partial(_dil_kernel, S=S),
        grid=(B, DIL_H),
        in_specs=[blk(col(COL_DQ)), blk(col(COL_DK)), blk(col(COL_DV)), blk(col(COL_GD)),
                  table, table,
                  pl.BlockSpec((1, DIL_HD), const), pl.BlockSpec((1, DIL_HD), const)],
        out_specs=blk(lambda b, h: (b, h)),
        out_shape=jax.ShapeDtypeStruct((B * S, DIL_H * DIL_HD), BF16),
        scratch_shapes=[pltpu.VMEM((S, DIL_HD), F32), pltpu.VMEM((S, DIL_HD), F32),
                        pltpu.VMEM((S, DIL_HD), BF16),
                        pltpu.VMEM((kv_rows, DIL_HD), BF16), pltpu.VMEM((kv_rows, 2 * DIL_HD), BF16),
                        pltpu.VMEM((len(DIL_CONFIGS), S, DIL_HD), F32),
                        pltpu.VMEM((len(DIL_CONFIGS), S, LANE), F32),
                        pltpu.VMEM((S // DIL_DIRECT_STRIDE, DIL_HD), F32)],
        compiler_params=_cparams(("parallel", "parallel")),
        name="dilated",
    )(u, u, u, u, cd, sd, qg, kg)


SRC_ALR, SRC_XBC, SRC_DT, SRC_CQKV, SRC_KR, SRC_DQKV = 3072, 3088, 4112, 4120, 4632, 4696
N_IN = 6232
PACK_RUNS = ((0, COL_XBC, 0), (SRC_XBC, SSD_CONV_DIM, COL_XBC), (SRC_CQKV, MLA_Q_LORA + MLA_KV_LORA, COL_CQKV),
             (SRC_DQKV, 3 * MIX_W, COL_DQ))


def _pack_kernel(wt_ref, o_ref):
    tc = wt_ref.shape[2]
    for src, width, dst in PACK_RUNS:
        for c0 in range(0, width, PACK_PIECE):
            o_ref[0, :, dst + c0:dst + c0 + PACK_PIECE] = (
                wt_ref[0, src + c0:src + c0 + PACK_PIECE, :].T.astype(BF16))
    small = jnp.concatenate(
        [wt_ref[0, SRC_KR:SRC_KR + MLA_ROPE, :], wt_ref[0, SRC_ALR:SRC_ALR + GLA_GATE_RANK, :],
         wt_ref[0, SRC_DT:SRC_DT + SSD_H, :], jnp.zeros((LANE - SM_DT - SSD_H, tc), F32)], axis=0)
    o_ref[0, :, COL_SMALL:COL_END] = small.T.astype(BF16)
    o_ref[0, :, COL_END:N_PACK] = jnp.zeros((tc, N_PACK - COL_END), BF16)


def _pack_w_in(w):
    depth, kdim, n_in = w.shape
    assert n_in == N_IN and kdim % PACK_COLS == 0 and all(width % PACK_PIECE == 0 for _, width, _ in PACK_RUNS)
    return pl.pallas_call(
        _pack_kernel,
        grid=(depth, kdim // PACK_COLS),
        in_specs=[pl.BlockSpec((1, N_IN, PACK_COLS), lambda l, i: (l, 0, i))],
        out_specs=pl.BlockSpec((1, PACK_COLS, N_PACK), lambda l, i: (l, i, 0)),
        out_shape=jax.ShapeDtypeStruct((depth, kdim, N_PACK), BF16),
        compiler_params=_cparams(("parallel", "parallel")),
        name="pack_w_in",
    )(jnp.swapaxes(w, 1, 2))


def _row(v, width=None, offset=0):
    v = v.astype(F32).reshape(1, -1)
    if width is None:
        return v
    return jnp.pad(v, ((0, 0), (offset, width - offset - v.shape[1])))


def _pad_heads(w, n_heads, real, padded):
    k = w.shape[0]
    w = w.reshape(k, n_heads, real)
    return jnp.pad(w, ((0, 0), (0, 0), (0, padded - real))).reshape(k, n_heads * padded)


def kernel(x, positions, ln_g, w_in, w_out, gla_gate_w2, gla_gate_b, gla_norm_g, ssd_conv_w, ssd_conv_b,
           ssd_dt_bias, ssd_A_log, ssd_D, ssd_norm_g, mla_q_norm_g, mla_kv_norm_g, mla_w_uq, mla_w_ukv,
           mla_q_head_g, mla_k_head_g, dil_q_g, dil_k_g):
    B, S, D = x.shape
    depth = w_in.shape[0]
    assert D == D_MODEL and S % SSD_ROWS == 0 and S % GLA_ROWS == 0
    assert all(S % (d * DIL_BLK) == 0 for _, d in DIL_CONFIGS)
    T = B * S
    cm, s1, s2, cd, sd = _rope_tables(positions)
    w_in_packed = _pack_w_in(w_in)
    w_out_bf16 = w_out.astype(BF16)
    xf = x.reshape(T, D)
    for l in range(depth):
        u = _inproj(xf, _row(ln_g[l]), w_in_packed, l)
        w2p = jnp.pad(gla_gate_w2[l], ((SM_LR, LANE - SM_LR - GLA_GATE_RANK), (0, 0))).astype(BF16)
        ya = _gla(u, w2p, _row(gla_gate_b[l]), _row(gla_norm_g[l]), B, S)
        yb = _ssd(u, ssd_conv_w[l].astype(F32), _row(ssd_conv_b[l]),
                  _row(ssd_dt_bias[l], LANE, SM_DT), _row(ssd_A_log[l], LANE, SM_DT),
                  _row(jnp.repeat(ssd_D[l], SSD_HD)), _row(ssd_norm_g[l]), B, S)
        q, k, v = _mla_prep(u, cm, s1, s2,
                            _pad_heads(mla_w_uq[l], MLA_H, MLA_QK, MLA_PITCH).astype(BF16),
                            mla_w_ukv[l].astype(BF16), _row(mla_q_norm_g[l]), _row(mla_kv_norm_g[l]),
                            _row(mla_q_head_g[l], MLA_PITCH), _row(mla_k_head_g[l], MLA_PITCH), B, S)
        yc = _flash(q, k, v, u, B, S)
        yd = _dil(u, cd, sd, _row(dil_q_g[l]), _row(dil_k_g[l]), B, S)
        xf = _outproj(xf, ya, yb, yc, yd, w_out_bf16, l)
    return xf.reshape(B, S, D)
```
